```python
import jax, jax.numpy as jnp
from jax import lax
import numpy as np

D_MODEL = 1024
BATCH = 4
SEQ = 8192
DEPTH = 1

CONV_DIM = 1024
CONV_WIDTH = 31
GLA_HEADS = 4
GLA_DK = D_MODEL // 2
GLA_DV = D_MODEL
GLA_HEAD_K = GLA_DK // GLA_HEADS
GLA_HEAD_V = GLA_DV // GLA_HEADS
GLA_GATE_RANK = 16
GLA_GATE_TAU = 16.0
GLA_CHUNK = 64
N_BRANCH = 2
D_FF = 4 * D_MODEL
EPS = 1e-6
IN_SIZES = (2 * CONV_DIM, GLA_DK, GLA_DK, GLA_DV, GLA_DV, GLA_GATE_RANK, N_BRANCH * D_MODEL)
D_IN = 2 * CONV_DIM + 2 * GLA_DK + 2 * GLA_DV + GLA_GATE_RANK + N_BRANCH * D_MODEL

kernel_name = "hybrid_conformer_conv_gla_block"


def rms_norm(x, g):
    xf = x.astype(jnp.float32)
    y = xf * lax.rsqrt(jnp.mean(xf * xf, axis=-1, keepdims=True) + EPS)
    return (y * g.astype(jnp.float32)).astype(x.dtype)


def layer_norm(x, g, b):
    xf = x.astype(jnp.float32)
    mu = jnp.mean(xf, axis=-1, keepdims=True)
    var = jnp.mean(jnp.square(xf - mu), axis=-1, keepdims=True)
    y = (xf - mu) * lax.rsqrt(var + EPS)
    return (y * g.astype(jnp.float32) + b.astype(jnp.float32)).astype(x.dtype)


def causal_depthwise_conv(u, w, bias):
    K = w.shape[0]
    u_pad = jnp.pad(u, ((0, 0), (K - 1, 0), (0, 0)))
    y = lax.conv_general_dilated(
        u_pad, w[:, None, :].astype(u.dtype), window_strides=(1,), padding='VALID',
        dimension_numbers=('NWC', 'WIO', 'NWC'), feature_group_count=u.shape[-1])
    return y + bias.astype(u.dtype)


def gla_chunked(q, k, v, log_a):
    B, S, H, dk = q.shape
    dv = v.shape[-1]
    C = GLA_CHUNK
    N = S // C

    def to_chunks(t):
        return t.astype(jnp.float32).reshape(B, N, C, H, t.shape[-1]).transpose(0, 3, 1, 2, 4)

    q, k, v, log_a = to_chunks(q), to_chunks(k), to_chunks(v), to_chunks(log_a)
    q = q * (dk ** -0.5)
    b = jnp.cumsum(log_a, axis=3)
    b_last = b[:, :, :, -1:, :]
    chunk_kv = jnp.einsum('bhncd,bhnce->bhnde', k * jnp.exp(b_last - b), v)
    chunk_decay = jnp.exp(b_last[:, :, :, 0, :])

    def step(state, inp):
        decay, kv = inp
        return decay[..., None] * state + kv, state

    init = jnp.zeros((B, H, dk, dv), jnp.float32)
    _, states = lax.scan(step, init, (chunk_decay.transpose(2, 0, 1, 3),
                                      chunk_kv.transpose(2, 0, 1, 3, 4)))
    states = states.transpose(1, 2, 0, 3, 4)
    o_inter = jnp.einsum('bhncd,bhnde->bhnce', q * jnp.exp(b), states)
    ref = b[:, :, :, C // 2:C // 2 + 1, :]
    scores = jnp.einsum('bhnid,bhnjd->bhnij', q * jnp.exp(b - ref), k * jnp.exp(ref - b))
    mask = jnp.tril(jnp.ones((C, C), dtype=bool))
    scores = jnp.where(mask, scores, 0.0)
    o_intra = jnp.einsum('bhnij,bhnje->bhnie', scores, v)
    o = o_inter + o_intra
    return o.transpose(0, 2, 3, 1, 4).reshape(B, S, H, dv)


def setup_inputs(seed: int = 0) -> dict:
    key = jax.random.key(seed)
    ks = jax.random.split(key, 24)

    def nrm(k, shape, scale):
        return jax.random.normal(k, shape, jnp.float32) * scale

    L = DEPTH
    return {
        "x": nrm(ks[0], (BATCH, SEQ, D_MODEL), 1.0),
        "c": nrm(ks[1], (BATCH, D_MODEL), 1.0),
        "w_ada": nrm(ks[2], (L, D_MODEL, 6 * D_MODEL), 0.5 * D_MODEL ** -0.5),
        "b_ada": nrm(ks[3], (L, 6 * D_MODEL), 0.02),
        "g_mix": 1.0 + nrm(ks[4], (L, D_MODEL), 0.02),
        "w_in": nrm(ks[5], (L, D_MODEL, D_IN), D_MODEL ** -0.5),
        "b_glu": nrm(ks[6], (L, 2 * CONV_DIM), 0.02),
        "w_dw": nrm(ks[7], (L, CONV_WIDTH, CONV_DIM), CONV_WIDTH ** -0.5),
        "b_dw": nrm(ks[8], (L, CONV_DIM), 0.02),
        "g_cln": 1.0 + nrm(ks[9], (L, CONV_DIM), 0.02),
        "b_cln": nrm(ks[10], (L, CONV_DIM), 0.02),
        "w_conv_out": nrm(ks[11], (L, CONV_DIM, D_MODEL), CONV_DIM ** -0.5),
        "b_conv_out": nrm(ks[12], (L, D_MODEL), 0.02),
        "w_a2": nrm(ks[13], (L, GLA_GATE_RANK, GLA_DK), GLA_GATE_RANK ** -0.5),
        "b_a2": nrm(ks[14], (L, GLA_DK), 0.1),
        "g_gla": 1.0 + nrm(ks[15], (L, GLA_DV), 0.02),
        "w_gla_out": nrm(ks[16], (L, GLA_DV, D_MODEL), GLA_DV ** -0.5),
        "w_o": nrm(ks[17], (L, D_MODEL, D_MODEL), D_MODEL ** -0.5),
        "g_mlp": 1.0 + nrm(ks[18], (L, D_MODEL), 0.02),
        "w_ff1": nrm(ks[19], (L, D_MODEL, D_FF), D_MODEL ** -0.5),
        "w_ff2": nrm(ks[20], (L, D_FF, D_MODEL), D_FF ** -0.5),
        "g_final": 1.0 + nrm(ks[21], (D_MODEL,), 0.02),
    }


def reference(x, c, w_ada, b_ada, g_mix, w_in, b_glu, w_dw, b_dw, g_cln, b_cln,
              w_conv_out, b_conv_out, w_a2, b_a2, g_gla, w_gla_out, w_o,
              g_mlp, w_ff1, w_ff2, g_final):
    B, S, D = x.shape
    split_idx = list(np.cumsum(IN_SIZES)[:-1])
    for l in range(DEPTH):
        mod = jax.nn.silu(c) @ w_ada[l] + b_ada[l]
        sh1, sc1, gt1, sh2, sc2, gt2 = [m[:, None, :] for m in jnp.split(mod, 6, axis=-1)]

        h = rms_norm(x, g_mix[l]) * (1.0 + sc1) + sh1
        p = h @ w_in[l]
        p_glu, p_q, p_k, p_v, p_g, p_lr, p_gate = jnp.split(p, split_idx, axis=-1)

        u_a, u_b = jnp.split(p_glu + b_glu[l], 2, axis=-1)
        u = u_a * jax.nn.sigmoid(u_b)
        u = causal_depthwise_conv(u, w_dw[l], b_dw[l])
        u = jax.nn.silu(layer_norm(u, g_cln[l], b_cln[l]))
        y_conv = u @ w_conv_out[l] + b_conv_out[l]

        log_a = jax.nn.log_sigmoid(p_lr @ w_a2[l] + b_a2[l]) / GLA_GATE_TAU
        q = p_q.reshape(B, S, GLA_HEADS, GLA_HEAD_K)
        k = p_k.reshape(B, S, GLA_HEADS, GLA_HEAD_K)
        v = p_v.reshape(B, S, GLA_HEADS, GLA_HEAD_V)
        la = log_a.reshape(B, S, GLA_HEADS, GLA_HEAD_K)
        o = gla_chunked(q, k, v, la)
        o = rms_norm(o, g_gla[l].reshape(GLA_HEADS, GLA_HEAD_V)).astype(x.dtype)
        o = o.reshape(B, S, GLA_DV) * jax.nn.silu(p_g)
        y_gla = o @ w_gla_out[l]

        gate_conv, gate_gla = jnp.split(jax.nn.sigmoid(p_gate), 2, axis=-1)
        merged = gate_conv * y_conv + gate_gla * y_gla
        x = x + gt1 * (merged @ w_o[l])

        h2 = rms_norm(x, g_mlp[l]) * (1.0 + sc2) + sh2
        f = jnp.square(jax.nn.relu(h2 @ w_ff1[l])) @ w_ff2[l]
        x = x + gt2 * f
    return rms_norm(x, g_final)
```

```python
import functools

import jax
import jax.numpy as jnp
import numpy as np
from jax import lax
from jax.experimental import pallas as pl
from jax.experimental.pallas import tpu as pltpu

D_MODEL = 1024
CONV_DIM = 1024
CONV_WIDTH = 31
GLA_HEADS = 4
GLA_DK = 512
GLA_DV = 1024
HEAD_K = GLA_DK // GLA_HEADS
HEAD_V = GLA_DV // GLA_HEADS
GATE_RANK = 16
GATE_TAU = 16.0
CHUNK = 64
D_FF = 4 * D_MODEL
EPS = 1e-6

LANES = 128
SUBLANES = 8
CONV_HALO = 32
VMEM_LIMIT = 56 * 1024 * 1024

F32 = jnp.float32
BF16 = jnp.bfloat16


def _const_spec(shape):
    nd = len(shape)
    return pl.BlockSpec(shape, lambda *_: (0,) * nd, pipeline_mode=pl.Buffered(1))


def _rms(x):
    return x * lax.rsqrt(jnp.mean(x * x, axis=-1, keepdims=True) + EPS)


def _sigmoid(x):
    return 1.0 / (1.0 + jnp.exp(-x))


def _mod_kernel(c_ref, w_ref, b_ref, o_ref):
    c = c_ref[...]
    s = c * _sigmoid(c)
    o_ref[...] = jnp.dot(s, w_ref[...], precision=lax.Precision.HIGHEST,
                         preferred_element_type=F32) + b_ref[...]


def _mod_call(c, w_ada, b_ada):
    B = c.shape[0]
    n = w_ada.shape[1]
    bn = 1024
    return pl.pallas_call(
        _mod_kernel,
        grid=(n // bn,),
        in_specs=[pl.BlockSpec((B, D_MODEL), lambda j: (0, 0)),
                  pl.BlockSpec((D_MODEL, bn), lambda j: (0, j)),
                  pl.BlockSpec((1, bn), lambda j: (0, j))],
        out_specs=pl.BlockSpec((B, bn), lambda j: (0, j)),
        out_shape=jax.ShapeDtypeStruct((B, n), F32),
        name="mod",
    )(c, w_ada, b_ada.reshape(1, n))


def _in_proj_kernel(x_ref, mod_ref, gmix_ref, wglu_ref, bglu_ref, wq_ref, wk_ref, wv_ref, wg_ref,
                    wlr_ref, wa2_ref, ba2_ref, wgate_ref,
                    u_ref, q_ref, k_ref, v_ref, sg_ref, la_ref, gc_ref, gg_ref):
    x = x_ref[0]
    sh1 = mod_ref[0, 0:1, :]
    sc1 = mod_ref[0, 1:2, :]
    h = (_rms(x) * gmix_ref[...]) * (1.0 + sc1) + sh1
    hb = h.astype(BF16)

    def proj(w_ref, lo, n):
        return jnp.dot(hb, w_ref[:, lo:lo + n], preferred_element_type=F32)

    nb = 512
    for j in range(CONV_DIM // nb):
        lo = j * nb
        a = proj(wglu_ref, lo, nb) + bglu_ref[:, lo:lo + nb]
        b = proj(wglu_ref, CONV_DIM + lo, nb) + bglu_ref[:, CONV_DIM + lo:CONV_DIM + lo + nb]
        u_ref[0, :, lo:lo + nb] = (a * _sigmoid(b)).astype(BF16)
    q_ref[0] = proj(wq_ref, 0, GLA_DK).astype(BF16)
    k_ref[0] = proj(wk_ref, 0, GLA_DK).astype(BF16)
    for j in range(GLA_DV // nb):
        lo = j * nb
        v_ref[0, :, lo:lo + nb] = proj(wv_ref, lo, nb).astype(BF16)
        g = proj(wg_ref, lo, nb)
        sg_ref[0, :, lo:lo + nb] = (g * _sigmoid(g)).astype(BF16)
    p_lr = proj(wlr_ref, 0, LANES).astype(BF16)
    z = jnp.dot(p_lr, wa2_ref[...], preferred_element_type=F32) + ba2_ref[...]
    la_ref[0] = (jnp.minimum(z, 0.0) - jnp.log(1.0 + jnp.exp(-jnp.abs(z)))) * (1.0 / GATE_TAU)
    for j in range(D_MODEL // nb):
        lo = j * nb
        gc_ref[0, :, lo:lo + nb] = _sigmoid(proj(wgate_ref, lo, nb)).astype(BF16)
        gg_ref[0, :, lo:lo + nb] = _sigmoid(proj(wgate_ref, D_MODEL + lo, nb)).astype(BF16)


def _in_proj_call(x, mod3, g_mix, wglu, bglu, wq, wk, wv, wg, wlr, wa2, ba2, wgate, tm):
    B, S, D = x.shape
    tok = lambda n: pl.BlockSpec((1, tm, n), lambda b, s: (b, s, 0))
    out_dims = (CONV_DIM, GLA_DK, GLA_DK, GLA_DV, GLA_DV, GLA_DK, D_MODEL, D_MODEL)
    out_dtypes = (BF16, BF16, BF16, BF16, BF16, F32, BF16, BF16)
    consts = (g_mix, wglu, bglu, wq, wk, wv, wg, wlr, wa2, ba2, wgate)
    return pl.pallas_call(
        _in_proj_kernel,
        grid=(B, S // tm),
        in_specs=[tok(D), pl.BlockSpec((1, 6, D), lambda b, s: (b, 0, 0))] + [_const_spec(a.shape) for a in consts],
        out_specs=[tok(n) for n in out_dims],
        out_shape=[jax.ShapeDtypeStruct((B, S, n), dt) for n, dt in zip(out_dims, out_dtypes)],
        compiler_params=pltpu.CompilerParams(dimension_semantics=("arbitrary", "arbitrary"),
                                             vmem_limit_bytes=VMEM_LIMIT),
        name="in_proj",
    )(x, mod3, *consts)


def _conv_kernel(u_ref, gc_ref, wdw_ref, bdw_ref, gln_ref, bln_ref, wout_ref, bout_ref, y_ref, ext_ref, cv_ref):
    tc = u_ref.shape[1]
    s = pl.program_id(1)

    @pl.when(s == 0)
    def _():
        ext_ref[0:CONV_HALO, :] = jnp.zeros((CONV_HALO, CONV_DIM), F32)

    @pl.when(s != 0)
    def _():
        ext_ref[0:CONV_HALO, :] = ext_ref[tc:tc + CONV_HALO, :]

    ext_ref[CONV_HALO:CONV_HALO + tc, :] = u_ref[0].astype(F32)

    rb = 64
    first = CONV_HALO - (CONV_WIDTH - 1)

    def row_block(r, carry):
        base = pl.multiple_of(r * rb, rb)
        for lb in range(CONV_DIM // LANES):
            lanes = slice(lb * LANES, (lb + 1) * LANES)
            win = ext_ref[pl.ds(base, rb + CONV_HALO), lanes]
            acc = jnp.zeros((rb, LANES), F32)
            for sh in range(SUBLANES):
                offs = [o for o in range(first, first + CONV_WIDTH) if o % SUBLANES == sh]
                span = max(offs) - sh + rb
                shifted = win[sh:sh + span, :]
                for o in offs:
                    k = o - first
                    acc = acc + wdw_ref[k:k + 1, lanes] * shifted[o - sh:o - sh + rb, :]
            cv_ref[pl.ds(base, rb), lanes] = acc
        return carry

    lax.fori_loop(0, tc // rb, row_block, 0)
    cv = cv_ref[...] + bdw_ref[...]
    mu = jnp.mean(cv, axis=-1, keepdims=True)
    cen = cv - mu
    var = jnp.mean(cen * cen, axis=-1, keepdims=True)
    yn = cen * lax.rsqrt(var + EPS) * gln_ref[...] + bln_ref[...]
    act = (yn * _sigmoid(yn)).astype(BF16)
    y = jnp.dot(act, wout_ref[...], preferred_element_type=F32) + bout_ref[...]
    y_ref[0] = (gc_ref[0].astype(F32) * y).astype(BF16)


def _conv_call(u, gc, wdw, bdw, gln, bln, wout, bout, tc):
    B, S, _ = u.shape
    tok = pl.BlockSpec((1, tc, CONV_DIM), lambda b, s: (b, s, 0))
    consts = (wdw, bdw, gln, bln, wout, bout)
    return pl.pallas_call(
        _conv_kernel,
        grid=(B, S // tc),
        in_specs=[tok, tok] + [_const_spec(a.shape) for a in consts],
        out_specs=tok,
        out_shape=jax.ShapeDtypeStruct((B, S, D_MODEL), BF16),
        scratch_shapes=[pltpu.VMEM((tc + CONV_HALO, CONV_DIM), F32), pltpu.VMEM((tc, CONV_DIM), F32)],
        compiler_params=pltpu.CompilerParams(dimension_semantics=("arbitrary", "arbitrary"),
                                             vmem_limit_bytes=VMEM_LIMIT),
        name="conv",
    )(u, gc, *consts)


def _split3(x):
    hi = x.astype(BF16)
    r1 = x - hi.astype(F32)
    mid = r1.astype(BF16)
    lo = (r1 - mid.astype(F32)).astype(BF16)
    return hi, mid, lo


def _gla_kernel(q_ref, k_ref, v_ref, la_ref, sg_ref, gg_ref, yc_ref, sel_ref, ggla_ref, wout_ref, m_ref,
                state_ref, kdec_ref, qb_ref, qm_ref, km_ref, dec_ref, on_ref):
    tg = q_ref.shape[1]
    s = pl.program_id(1)

    @pl.when(s == 0)
    def _():
        state_ref[...] = jnp.zeros_like(state_ref)

    parts = _split3(la_ref[0])

    def sel(i):
        m = sel_ref[i]
        return sum(jnp.dot(m, p, preferred_element_type=F32) for p in parts)

    b = sel(0)
    b_end = sel(1)
    b_mid = sel(2)
    q = q_ref[0].astype(F32) * (HEAD_K ** -0.5)
    k = k_ref[0].astype(F32)
    kdec_ref[...] = (k * jnp.exp(b_end - b)).astype(BF16)
    qb_ref[...] = (q * jnp.exp(b)).astype(BF16)
    qm_ref[...] = (q * jnp.exp(b - b_mid)).astype(BF16)
    km_ref[...] = (k * jnp.exp(b_mid - b)).astype(BF16)
    dec_ref[...] = jnp.exp(b_end)

    ri = lax.broadcasted_iota(jnp.int32, (CHUNK, CHUNK), 0)
    ci = lax.broadcasted_iota(jnp.int32, (CHUNK, CHUNK), 1)
    causal = ci <= ri

    def chunk_body(c, carry):
        r0 = pl.multiple_of(c * CHUNK, CHUNK)
        rows = pl.ds(r0, CHUNK)
        for h in range(GLA_HEADS):
            kc = slice(h * HEAD_K, (h + 1) * HEAD_K)
            vc = slice(h * HEAD_V, (h + 1) * HEAD_V)
            vh = v_ref[0, rows, vc]
            st = state_ref[h]
            scores = lax.dot_general(qm_ref[rows, kc], km_ref[rows, kc], (((1,), (1,)), ((), ())),
                                     preferred_element_type=F32)
            scores = jnp.where(causal, scores, 0.0).astype(BF16)
            o = jnp.dot(scores, vh, preferred_element_type=F32)
            o = o + lax.dot_general(qb_ref[rows, kc], st.astype(BF16), (((1,), (1,)), ((), ())),
                                    preferred_element_type=F32)
            kv = lax.dot_general(vh, kdec_ref[rows, kc], (((0,), (0,)), ((), ())),
                                 preferred_element_type=F32)
            state_ref[h] = st * dec_ref[pl.ds(r0, 1), kc] + kv
            on = _rms(o) * ggla_ref[:, vc] * sg_ref[0, rows, vc].astype(F32)
            on_ref[rows, vc] = on.astype(BF16)
        return carry

    lax.fori_loop(0, tg // CHUNK, chunk_body, 0)
    y = jnp.dot(on_ref[...], wout_ref[...], preferred_element_type=F32)
    m_ref[0] = (yc_ref[0].astype(F32) + gg_ref[0].astype(F32) * y).astype(BF16)


def _chunk_selectors(tg):
    row = np.arange(tg)[:, None]
    col = np.arange(tg)[None, :]
    same = (row // CHUNK) == (col // CHUNK)
    start = (row // CHUNK) * CHUNK
    sel = np.stack([same & (col <= row), same, same & (col <= start + CHUNK // 2)])
    return jnp.asarray(sel, dtype=BF16)


def _gla_call(q, k, v, la, sg, gg, yc, ggla, wout, tg):
    B, S, _ = q.shape
    tok = lambda n: pl.BlockSpec((1, tg, n), lambda b, s: (b, s, 0))
    consts = (_chunk_selectors(tg), ggla, wout)
    return pl.pallas_call(
        _gla_kernel,
        grid=(B, S // tg),
        in_specs=[tok(GLA_DK), tok(GLA_DK), tok(GLA_DV), tok(GLA_DK), tok(GLA_DV), tok(D_MODEL), tok(D_MODEL)]
                 + [_const_spec(a.shape) for a in consts],
        out_specs=tok(D_MODEL),
        out_shape=jax.ShapeDtypeStruct((B, S, D_MODEL), BF16),
        scratch_shapes=[pltpu.VMEM((GLA_HEADS, HEAD_V, HEAD_K), F32),
                        pltpu.VMEM((tg, GLA_DK), BF16), pltpu.VMEM((tg, GLA_DK), BF16),
                        pltpu.VMEM((tg, GLA_DK), BF16), pltpu.VMEM((tg, GLA_DK), BF16),
                        pltpu.VMEM((tg, GLA_DK), F32), pltpu.VMEM((tg, GLA_DV), BF16)],
        compiler_params=pltpu.CompilerParams(dimension_semantics=("arbitrary", "arbitrary"),
                                             vmem_limit_bytes=VMEM_LIMIT),
        name="gla",
    )(q, k, v, la, sg, gg, yc, *consts)


def _mlp_kernel(x_ref, m_ref, mod_ref, wo_ref, gmlp_ref, w1_ref, w2_ref, gfin_ref, o_ref):
    gt1 = mod_ref[0, 2:3, :]
    sh2 = mod_ref[0, 3:4, :]
    sc2 = mod_ref[0, 4:5, :]
    gt2 = mod_ref[0, 5:6, :]
    x1 = x_ref[0] + gt1 * jnp.dot(m_ref[0], wo_ref[...], preferred_element_type=F32)
    h2 = ((_rms(x1) * gmlp_ref[...]) * (1.0 + sc2) + sh2).astype(BF16)
    nb = 1024
    f = jnp.zeros_like(x1)
    for j in range(D_FF // nb):
        t = jnp.dot(h2, w1_ref[:, j * nb:(j + 1) * nb], preferred_element_type=F32)
        t = jnp.maximum(t, 0.0)
        f = f + jnp.dot((t * t).astype(BF16), w2_ref[j * nb:(j + 1) * nb, :], preferred_element_type=F32)
    x2 = x1 + gt2 * f
    o_ref[0] = _rms(x2) * gfin_ref[...]


def _mlp_call(x, m, mod3, wo, gmlp, w1, w2, gfin, tm):
    B, S, D = x.shape
    tok = pl.BlockSpec((1, tm, D), lambda b, s: (b, s, 0))
    consts_a = (wo, gmlp, w1, w2, gfin)
    return pl.pallas_call(
        _mlp_kernel,
        grid=(B, S // tm),
        in_specs=[tok, tok, pl.BlockSpec((1, 6, D), lambda b, s: (b, 0, 0))] + [_const_spec(a.shape) for a in consts_a],
        out_specs=tok,
        out_shape=jax.ShapeDtypeStruct((B, S, D), F32),
        compiler_params=pltpu.CompilerParams(dimension_semantics=("arbitrary", "arbitrary"),
                                             vmem_limit_bytes=VMEM_LIMIT),
        name="mlp",
    )(x, m, mod3, *consts_a)


def _tile(S, want):
    t = min(S, want)
    assert S % t == 0 and t % CHUNK == 0, (S, t)
    return t


def kernel(x, c, w_ada, b_ada, g_mix, w_in, b_glu, w_dw, b_dw, g_cln, b_cln, w_conv_out, b_conv_out,
           w_a2, b_a2, g_gla, w_gla_out, w_o, g_mlp, w_ff1, w_ff2, g_final):
    B, S, D = x.shape
    assert D == D_MODEL and w_ada.shape[0] == 1
    row = lambda a: a.reshape(1, -1)

    mod3 = _mod_call(c, w_ada[0], b_ada[0]).reshape(B, 6, D)

    wi = w_in[0].astype(BF16)
    o = 0
    wglu = wi[:, o:o + 2 * CONV_DIM]; o += 2 * CONV_DIM
    wq = wi[:, o:o + GLA_DK]; o += GLA_DK
    wk = wi[:, o:o + GLA_DK]; o += GLA_DK
    wv = wi[:, o:o + GLA_DV]; o += GLA_DV
    wg = wi[:, o:o + GLA_DV]; o += GLA_DV
    wlr = jnp.pad(wi[:, o:o + GATE_RANK], ((0, 0), (0, LANES - GATE_RANK))); o += GATE_RANK
    wgate = wi[:, o:o + 2 * D_MODEL]
    wa2 = jnp.pad(w_a2[0].astype(BF16), ((0, LANES - GATE_RANK), (0, 0)))

    u, q, k, v, sg, la, gc, gg = _in_proj_call(
        x, mod3, row(g_mix[0]), wglu, row(b_glu[0]), wq, wk, wv, wg, wlr, wa2, row(b_a2[0]), wgate,
        tm=_tile(S, 512))
    yc = _conv_call(u, gc, w_dw[0], row(b_dw[0]), row(g_cln[0]), row(b_cln[0]),
                    w_conv_out[0].astype(BF16), row(b_conv_out[0]), tc=_tile(S, 512))
    m = _gla_call(q, k, v, la, sg, gg, yc, row(g_gla[0]), w_gla_out[0].astype(BF16), tg=_tile(S, 256))
    return _mlp_call(x, m, mod3, w_o[0].astype(BF16), row(g_mlp[0]), w_ff1[0].astype(BF16),
                     w_ff2[0].astype(BF16), row(g_final), tm=_tile(S, 512))
```

```python
import functools

import jax
import jax.numpy as jnp
import numpy as np
from jax import lax
from jax.experimental import pallas as pl
from jax.experimental.pallas import tpu as pltpu

D_MODEL = 1024
CONV_DIM = 1024
CONV_WIDTH = 31
GLA_HEADS = 4
GLA_DK = 512
GLA_DV = 1024
HEAD_K = GLA_DK // GLA_HEADS
HEAD_V = GLA_DV // GLA_HEADS
GATE_RANK = 16
GATE_TAU = 16.0
CHUNK = 64
D_FF = 4 * D_MODEL
EPS = 1e-6

LANES = 128
SUBLANES = 8
CONV_HALO = 32
VMEM_LIMIT = 56 * 1024 * 1024

F32 = jnp.float32
BF16 = jnp.bfloat16


def _const_spec(shape):
    nd = len(shape)
    return pl.BlockSpec(shape, lambda *_: (0,) * nd, pipeline_mode=pl.Buffered(1))


def _rms(x):
    return x * lax.rsqrt(jnp.mean(x * x, axis=-1, keepdims=True) + EPS)


def _sigmoid(x):
    return 1.0 / (1.0 + jnp.exp(-x))


def _mod_kernel(c_ref, w_ref, b_ref, o_ref):
    c = c_ref[...]
    s = c * _sigmoid(c)
    o_ref[...] = jnp.dot(s, w_ref[...], precision=lax.Precision.HIGHEST,
                         preferred_element_type=F32) + b_ref[...]


def _mod_call(c, w_ada, b_ada):
    B = c.shape[0]
    n = w_ada.shape[1]
    bn = 1024
    return pl.pallas_call(
        _mod_kernel,
        grid=(n // bn,),
        in_specs=[pl.BlockSpec((B, D_MODEL), lambda j: (0, 0)),
                  pl.BlockSpec((D_MODEL, bn), lambda j: (0, j)),
                  pl.BlockSpec((1, bn), lambda j: (0, j))],
        out_specs=pl.BlockSpec((B, bn), lambda j: (0, j)),
        out_shape=jax.ShapeDtypeStruct((B, n), F32),
        name="mod",
    )(c, w_ada, b_ada.reshape(1, n))


def _in_proj_kernel(x_ref, mod_ref, gmix_ref, wglu_ref, bglu_ref, wq_ref, wk_ref, wv_ref, wg_ref,
                    wlr_ref, wa2_ref, ba2_ref, wgate_ref,
                    u_ref, q_ref, k_ref, v_ref, sg_ref, la_ref, gc_ref, gg_ref):
    x = x_ref[0]
    sh1 = mod_ref[0, 0:1, :]
    sc1 = mod_ref[0, 1:2, :]
    h = (_rms(x) * gmix_ref[...]) * (1.0 + sc1) + sh1
    hb = h.astype(BF16)

    def proj(w_ref, lo, n):
        return jnp.dot(hb, w_ref[:, lo:lo + n], preferred_element_type=F32)

    nb = 512
    for j in range(CONV_DIM // nb):
        lo = j * nb
        a = proj(wglu_ref, lo, nb) + bglu_ref[:, lo:lo + nb]
        b = proj(wglu_ref, CONV_DIM + lo, nb) + bglu_ref[:, CONV_DIM + lo:CONV_DIM + lo + nb]
        u_ref[0, :, lo:lo + nb] = (a * _sigmoid(b)).astype(BF16)
    q_ref[0] = proj(wq_ref, 0, GLA_DK).astype(BF16)
    k_ref[0] = proj(wk_ref, 0, GLA_DK).astype(BF16)
    for j in range(GLA_DV // nb):
        lo = j * nb
        v_ref[0, :, lo:lo + nb] = proj(wv_ref, lo, nb).astype(BF16)
        g = proj(wg_ref, lo, nb)
        sg_ref[0, :, lo:lo + nb] = (g * _sigmoid(g)).astype(BF16)
    p_lr = proj(wlr_ref, 0, LANES).astype(BF16)
    z = jnp.dot(p_lr, wa2_ref[...], preferred_element_type=F32) + ba2_ref[...]
    la_ref[0] = (jnp.minimum(z, 0.0) - jnp.log(1.0 + jnp.exp(-jnp.abs(z)))) * (1.0 / GATE_TAU)
    for j in range(D_MODEL // nb):
        lo = j * nb
        gc_ref[0, :, lo:lo + nb] = _sigmoid(proj(wgate_ref, lo, nb)).astype(BF16)
        gg_ref[0, :, lo:lo + nb] = _sigmoid(proj(wgate_ref, D_MODEL + lo, nb)).astype(BF16)


def _in_proj_call(x, mod3, g_mix, wglu, bglu, wq, wk, wv, wg, wlr, wa2, ba2, wgate, tm):
    B, S, D = x.shape
    tok = lambda n: pl.BlockSpec((1, tm, n), lambda b, s: (b, s, 0))
    out_dims = (CONV_DIM, GLA_DK, GLA_DK, GLA_DV, GLA_DV, GLA_DK, D_MODEL, D_MODEL)
    out_dtypes = (BF16, BF16, BF16, BF16, BF16, F32, BF16, BF16)
    consts = (g_mix, wglu, bglu, wq, wk, wv, wg, wlr, wa2, ba2, wgate)
    return pl.pallas_call(
        _in_proj_kernel,
        grid=(B, S // tm),
        in_specs=[tok(D), pl.BlockSpec((1, 6, D), lambda b, s: (b, 0, 0))] + [_const_spec(a.shape) for a in consts],
        out_specs=[tok(n) for n in out_dims],
        out_shape=[jax.ShapeDtypeStruct((B, S, n), dt) for n, dt in zip(out_dims, out_dtypes)],
        compiler_params=pltpu.CompilerParams(dimension_semantics=("arbitrary", "arbitrary"),
                                             vmem_limit_bytes=VMEM_LIMIT),
        name="in_proj",
    )(x, mod3, *consts)


def _conv_kernel(u_ref, gc_ref, wdw_ref, bdw_ref, gln_ref, bln_ref, wout_ref, bout_ref, y_ref, ext_ref, cv_ref):
    tc = u_ref.shape[1]
    s = pl.program_id(1)

    @pl.when(s == 0)
    def _():
        ext_ref[0:CONV_HALO, :] = jnp.zeros((CONV_HALO, CONV_DIM), F32)

    @pl.when(s != 0)
    def _():
        ext_ref[0:CONV_HALO, :] = ext_ref[tc:tc + CONV_HALO, :]

    ext_ref[CONV_HALO:CONV_HALO + tc, :] = u_ref[0].astype(F32)

    rb = 64
    first = CONV_HALO - (CONV_WIDTH - 1)

    def row_block(r, carry):
        base = pl.multiple_of(r * rb, rb)
        for lb in range(CONV_DIM // LANES):
            lanes = slice(lb * LANES, (lb + 1) * LANES)
            win = ext_ref[pl.ds(base, rb + CONV_HALO), lanes]
            acc = jnp.zeros((rb, LANES), F32)
            for sh in range(SUBLANES):
                offs = [o for o in range(first, first + CONV_WIDTH) if o % SUBLANES == sh]
                shifted = win if sh == 0 else pltpu.roll(win, rb + CONV_HALO - sh, 0)
                for o in offs:
                    k = o - first
                    acc = acc + wdw_ref[k:k + 1, lanes] * shifted[o - sh:o - sh + rb, :]
            cv_ref[pl.ds(base, rb), lanes] = acc
        return carry

    lax.fori_loop(0, tc // rb, row_block, 0)
    cv = cv_ref[...] + bdw_ref[...]
    mu = jnp.mean(cv, axis=-1, keepdims=True)
    cen = cv - mu
    var = jnp.mean(cen * cen, axis=-1, keepdims=True)
    yn = cen * lax.rsqrt(var + EPS) * gln_ref[...] + bln_ref[...]
    act = (yn * _sigmoid(yn)).astype(BF16)
    y = jnp.dot(act, wout_ref[...], preferred_element_type=F32) + bout_ref[...]
    y_ref[0] = (gc_ref[0].astype(F32) * y).astype(BF16)


def _conv_call(u, gc, wdw, bdw, gln, bln, wout, bout, tc):
    B, S, _ = u.shape
    tok = pl.BlockSpec((1, tc, CONV_DIM), lambda b, s: (b, s, 0))
    consts = (wdw, bdw, gln, bln, wout, bout)
    return pl.pallas_call(
        _conv_kernel,
        grid=(B, S // tc),
        in_specs=[tok, tok] + [_const_spec(a.shape) for a in consts],
        out_specs=tok,
        out_shape=jax.ShapeDtypeStruct((B, S, D_MODEL), BF16),
        scratch_shapes=[pltpu.VMEM((tc + CONV_HALO, CONV_DIM), F32), pltpu.VMEM((tc, CONV_DIM), F32)],
        compiler_params=pltpu.CompilerParams(dimension_semantics=("arbitrary", "arbitrary"),
                                             vmem_limit_bytes=VMEM_LIMIT),
        name="conv",
    )(u, gc, *consts)


def _split3(x):
    hi = x.astype(BF16)
    r1 = x - hi.astype(F32)
    mid = r1.astype(BF16)
    lo = (r1 - mid.astype(F32)).astype(BF16)
    return hi, mid, lo


def _gla_kernel(q_ref, k_ref, v_ref, la_ref, sg_ref, gg_ref, yc_ref, sel_ref, ggla_ref, wout_ref, m_ref,
                state_ref, b_ref, on_ref):
    tg = q_ref.shape[1]
    n_chunks = tg // CHUNK
    s = pl.program_id(1)

    @pl.when(s == 0)
    def _():
        state_ref[...] = jnp.zeros_like(state_ref)

    b_ref[...] = sum(jnp.dot(sel_ref[...], p, preferred_element_type=F32) for p in _split3(la_ref[0]))

    kdec, qb, qm, km, dec = [], [], [], [], []
    for c in range(n_chunks):
        rows = slice(c * CHUNK, (c + 1) * CHUNK)
        b = b_ref[rows, :]
        b_end = b_ref[(c + 1) * CHUNK - 1:(c + 1) * CHUNK, :]
        b_mid = b_ref[c * CHUNK + CHUNK // 2:c * CHUNK + CHUNK // 2 + 1, :]
        q = q_ref[0, rows, :].astype(F32) * (HEAD_K ** -0.5)
        k = k_ref[0, rows, :].astype(F32)
        kdec.append((k * jnp.exp(b_end - b)).astype(BF16))
        qb.append((q * jnp.exp(b)).astype(BF16))
        qm.append((q * jnp.exp(b - b_mid)).astype(BF16))
        km.append((k * jnp.exp(b_mid - b)).astype(BF16))
        dec.append(jnp.exp(b_end))
    qm = jnp.concatenate(qm, axis=0)
    km = jnp.concatenate(km, axis=0)

    ri = lax.broadcasted_iota(jnp.int32, (tg, tg), 0)
    ci = lax.broadcasted_iota(jnp.int32, (tg, tg), 1)
    keep = ((ri // CHUNK) == (ci // CHUNK)) & (ci <= ri)
    nt = (((1,), (1,)), ((), ()))
    tn = (((0,), (0,)), ((), ()))

    for h in range(GLA_HEADS):
        kc = slice(h * HEAD_K, (h + 1) * HEAD_K)
        vc = slice(h * HEAD_V, (h + 1) * HEAD_V)
        vh = v_ref[0, :, vc]
        scores = lax.dot_general(qm[:, kc], km[:, kc], nt, preferred_element_type=F32)
        o = jnp.dot(jnp.where(keep, scores, 0.0).astype(BF16), vh, preferred_element_type=F32)
        st = state_ref[h]
        inter = []
        for c in range(n_chunks):
            rows = slice(c * CHUNK, (c + 1) * CHUNK)
            inter.append(lax.dot_general(qb[c][:, kc], st.astype(BF16), nt, preferred_element_type=F32))
            kv = lax.dot_general(vh[rows, :], kdec[c][:, kc], tn, preferred_element_type=F32)
            st = st * dec[c][:, kc] + kv
        state_ref[h] = st
        o = o + jnp.concatenate(inter, axis=0)
        on = _rms(o) * ggla_ref[:, vc] * sg_ref[0, :, vc].astype(F32)
        on_ref[:, vc] = on.astype(BF16)

    y = jnp.dot(on_ref[...], wout_ref[...], preferred_element_type=F32)
    m_ref[0] = (yc_ref[0].astype(F32) + gg_ref[0].astype(F32) * y).astype(BF16)


def _chunk_selectors(tg):
    row = np.arange(tg)[:, None]
    col = np.arange(tg)[None, :]
    return jnp.asarray(((row // CHUNK) == (col // CHUNK)) & (col <= row), dtype=BF16)


def _gla_call(q, k, v, la, sg, gg, yc, ggla, wout, tg):
    B, S, _ = q.shape
    tok = lambda n: pl.BlockSpec((1, tg, n), lambda b, s: (b, s, 0))
    consts = (_chunk_selectors(tg), ggla, wout)
    return pl.pallas_call(
        _gla_kernel,
        grid=(B, S // tg),
        in_specs=[tok(GLA_DK), tok(GLA_DK), tok(GLA_DV), tok(GLA_DK), tok(GLA_DV), tok(D_MODEL), tok(D_MODEL)]
                 + [_const_spec(a.shape) for a in consts],
        out_specs=tok(D_MODEL),
        out_shape=jax.ShapeDtypeStruct((B, S, D_MODEL), BF16),
        scratch_shapes=[pltpu.VMEM((GLA_HEADS, HEAD_V, HEAD_K), F32),
                        pltpu.VMEM((tg, GLA_DK), F32), pltpu.VMEM((tg, GLA_DV), BF16)],
        compiler_params=pltpu.CompilerParams(dimension_semantics=("arbitrary", "arbitrary"),
                                             vmem_limit_bytes=VMEM_LIMIT),
        name="gla",
    )(q, k, v, la, sg, gg, yc, *consts)


def _mlp_kernel(x_ref, m_ref, mod_ref, wo_ref, gmlp_ref, w1_ref, w2_ref, gfin_ref, o_ref):
    gt1 = mod_ref[0, 2:3, :]
    sh2 = mod_ref[0, 3:4, :]
    sc2 = mod_ref[0, 4:5, :]
    gt2 = mod_ref[0, 5:6, :]
    x1 = x_ref[0] + gt1 * jnp.dot(m_ref[0], wo_ref[...], preferred_element_type=F32)
    h2 = ((_rms(x1) * gmlp_ref[...]) * (1.0 + sc2) + sh2).astype(BF16)
    nb = 1024
    f = jnp.zeros_like(x1)
    for j in range(D_FF // nb):
        t = jnp.dot(h2, w1_ref[:, j * nb:(j + 1) * nb], preferred_element_type=F32)
        t = jnp.maximum(t, 0.0)
        f = f + jnp.dot((t * t).astype(BF16), w2_ref[j * nb:(j + 1) * nb, :], preferred_element_type=F32)
    x2 = x1 + gt2 * f
    o_ref[0] = _rms(x2) * gfin_ref[...]


def _mlp_call(x, m, mod3, wo, gmlp, w1, w2, gfin, tm):
    B, S, D = x.shape
    tok = pl.BlockSpec((1, tm, D), lambda b, s: (b, s, 0))
    consts_a = (wo, gmlp, w1, w2, gfin)
    return pl.pallas_call(
        _mlp_kernel,
        grid=(B, S // tm),
        in_specs=[tok, tok, pl.BlockSpec((1, 6, D), lambda b, s: (b, 0, 0))] + [_const_spec(a.shape) for a in consts_a],
        out_specs=tok,
        out_shape=jax.ShapeDtypeStruct((B, S, D), F32),
        compiler_params=pltpu.CompilerParams(dimension_semantics=("arbitrary", "arbitrary"),
                                             vmem_limit_bytes=VMEM_LIMIT),
        name="mlp",
    )(x, m, mod3, *consts_a)


def _tile(S, want):
    t = min(S, want)
    assert S % t == 0 and t % CHUNK == 0, (S, t)
    return t


def kernel(x, c, w_ada, b_ada, g_mix, w_in, b_glu, w_dw, b_dw, g_cln, b_cln, w_conv_out, b_conv_out,
           w_a2, b_a2, g_gla, w_gla_out, w_o, g_mlp, w_ff1, w_ff2, g_final):
    B, S, D = x.shape
    assert D == D_MODEL and w_ada.shape[0] == 1
    row = lambda a: a.reshape(1, -1)

    mod3 = _mod_call(c, w_ada[0], b_ada[0]).reshape(B, 6, D)

    wi = w_in[0].astype(BF16)
    o = 0
    wglu = wi[:, o:o + 2 * CONV_DIM]; o += 2 * CONV_DIM
    wq = wi[:, o:o + GLA_DK]; o += GLA_DK
    wk = wi[:, o:o + GLA_DK]; o += GLA_DK
    wv = wi[:, o:o + GLA_DV]; o += GLA_DV
    wg = wi[:, o:o + GLA_DV]; o += GLA_DV
    wlr = jnp.pad(wi[:, o:o + GATE_RANK], ((0, 0), (0, LANES - GATE_RANK))); o += GATE_RANK
    wgate = wi[:, o:o + 2 * D_MODEL]
    wa2 = jnp.pad(w_a2[0].astype(BF16), ((0, LANES - GATE_RANK), (0, 0)))

    u, q, k, v, sg, la, gc, gg = _in_proj_call(
        x, mod3, row(g_mix[0]), wglu, row(b_glu[0]), wq, wk, wv, wg, wlr, wa2, row(b_a2[0]), wgate,
        tm=_tile(S, 512))
    yc = _conv_call(u, gc, w_dw[0], row(b_dw[0]), row(g_cln[0]), row(b_cln[0]),
                    w_conv_out[0].astype(BF16), row(b_conv_out[0]), tc=_tile(S, 512))
    m = _gla_call(q, k, v, la, sg, gg, yc, row(g_gla[0]), w_gla_out[0].astype(BF16), tg=_tile(S, 256))
    return _mlp_call(x, m, mod3, w_o[0].astype(BF16), row(g_mlp[0]), w_ff1[0].astype(BF16),
                     w_ff2[0].astype(BF16), row(g_final), tm=_tile(S, 512))
```

```python
import functools

import jax
import jax.numpy as jnp
import numpy as np
from jax import lax
from jax.experimental import pallas as pl
from jax.experimental.pallas import tpu as pltpu

D_MODEL = 1024
CONV_DIM = 1024
CONV_WIDTH = 31
GLA_HEADS = 4
GLA_DK = 512
GLA_DV = 1024
HEAD_K = GLA_DK // GLA_HEADS
HEAD_V = GLA_DV // GLA_HEADS
GATE_RANK = 16
GATE_TAU = 16.0
CHUNK = 64
D_FF = 4 * D_MODEL
EPS = 1e-6

LANES = 128
SUBLANES = 8
CONV_HALO = 32
CONV_ROWS = 64
VMEM_LIMIT = 56 * 1024 * 1024

F32 = jnp.float32
BF16 = jnp.bfloat16


def _const_spec(shape):
    nd = len(shape)
    return pl.BlockSpec(shape, lambda *_: (0,) * nd, pipeline_mode=pl.Buffered(1))


def _rms(x):
    return x * lax.rsqrt(jnp.mean(x * x, axis=-1, keepdims=True) + EPS)


def _sigmoid(x):
    return 1.0 / (1.0 + jnp.exp(-x))


def _mod_kernel(c_ref, w_ref, b_ref, o_ref):
    c = c_ref[...]
    s = c * _sigmoid(c)
    o_ref[...] = jnp.dot(s, w_ref[...], precision=lax.Precision.HIGHEST,
                         preferred_element_type=F32) + b_ref[...]


def _mod_call(c, w_ada, b_ada):
    B = c.shape[0]
    n = w_ada.shape[1]
    bn = 1024
    return pl.pallas_call(
        _mod_kernel,
        grid=(n // bn,),
        in_specs=[pl.BlockSpec((B, D_MODEL), lambda j: (0, 0)),
                  pl.BlockSpec((D_MODEL, bn), lambda j: (0, j)),
                  pl.BlockSpec((1, bn), lambda j: (0, j))],
        out_specs=pl.BlockSpec((B, bn), lambda j: (0, j)),
        out_shape=jax.ShapeDtypeStruct((B, n), F32),
        name="mod",
    )(c, w_ada, b_ada.reshape(1, n))


def _conv_block(ext_ref, wdw_ref, cv_ref, base, lb):
    first = CONV_HALO - (CONV_WIDTH - 1)
    lanes = slice(lb * LANES, (lb + 1) * LANES)
    acc = jnp.zeros((CONV_ROWS, LANES), F32)
    for k in range(CONV_WIDTH):
        acc = acc + wdw_ref[k:k + 1, lanes] * ext_ref[lb, base + first + k:base + first + k + CONV_ROWS, :]
    cv_ref[base:base + CONV_ROWS, lanes] = acc


def _in_proj_kernel(x_ref, mod_ref, gmix_ref, wglu_ref, bglu_ref, wq_ref, wk_ref, wv_ref, wg_ref,
                    wlr_ref, wa2_ref, ba2_ref, wgate_ref, wdw_ref, bdw_ref, gln_ref, bln_ref, wco_ref, bco_ref,
                    q_ref, k_ref, v_ref, sg_ref, la_ref, gg_ref, yc_ref, ext_ref, cv_ref, hb_ref, gc_ref):
    tm = x_ref.shape[1]
    s = pl.program_id(1)
    n_lb = CONV_DIM // LANES

    @pl.when(s == 0)
    def _():
        ext_ref[:, 0:CONV_HALO, :] = jnp.zeros((n_lb, CONV_HALO, LANES), F32)

    @pl.when(s != 0)
    def _():
        ext_ref[:, 0:CONV_HALO, :] = ext_ref[:, tm:tm + CONV_HALO, :]

    x = x_ref[0]
    sh1 = mod_ref[0, 0:1, :]
    sc1 = mod_ref[0, 1:2, :]
    h = (_rms(x) * gmix_ref[...]) * (1.0 + sc1) + sh1
    hb_ref[...] = h.astype(BF16)

    def proj(w_ref, lo, n):
        return jnp.dot(hb_ref[...], w_ref[:, lo:lo + n], preferred_element_type=F32)

    nb = 512
    for j in range(CONV_DIM // nb):
        lo = j * nb
        a = proj(wglu_ref, lo, nb) + bglu_ref[:, lo:lo + nb]
        b = proj(wglu_ref, CONV_DIM + lo, nb) + bglu_ref[:, CONV_DIM + lo:CONV_DIM + lo + nb]
        u = a * _sigmoid(b)
        for i in range(nb // LANES):
            ext_ref[lo // LANES + i, CONV_HALO:CONV_HALO + tm, :] = u[:, i * LANES:(i + 1) * LANES]

    def q_seg():
        q_ref[0] = proj(wq_ref, 0, GLA_DK).astype(BF16)

    def k_seg():
        k_ref[0] = proj(wk_ref, 0, GLA_DK).astype(BF16)

    def v_seg(lo):
        v_ref[0, :, lo:lo + nb] = proj(wv_ref, lo, nb).astype(BF16)

    def g_seg(lo):
        g = proj(wg_ref, lo, nb)
        sg_ref[0, :, lo:lo + nb] = (g * _sigmoid(g)).astype(BF16)

    def la_seg():
        p_lr = proj(wlr_ref, 0, LANES).astype(BF16)
        z = jnp.dot(p_lr, wa2_ref[...], preferred_element_type=F32) + ba2_ref[...]
        la_ref[0] = (jnp.minimum(z, 0.0) - jnp.log(1.0 + jnp.exp(-jnp.abs(z)))) * (1.0 / GATE_TAU)

    def gg_seg(lo):
        gg_ref[0, :, lo:lo + nb] = _sigmoid(proj(wgate_ref, D_MODEL + lo, nb)).astype(BF16)

    def gc_seg(lo):
        gc_ref[:, lo:lo + nb] = _sigmoid(proj(wgate_ref, lo, nb)).astype(BF16)

    segments = [q_seg, k_seg, functools.partial(v_seg, 0), functools.partial(v_seg, nb),
                functools.partial(g_seg, 0), functools.partial(g_seg, nb), la_seg,
                functools.partial(gg_seg, 0), functools.partial(gg_seg, nb),
                functools.partial(gc_seg, 0), functools.partial(gc_seg, nb)]
    conv_blocks = [(r * CONV_ROWS, lb) for r in range(tm // CONV_ROWS) for lb in range(n_lb)]
    bounds = [round(i * len(conv_blocks) / len(segments)) for i in range(len(segments) + 1)]
    for i, seg in enumerate(segments):
        seg()
        for base, lb in conv_blocks[bounds[i]:bounds[i + 1]]:
            _conv_block(ext_ref, wdw_ref, cv_ref, base, lb)

    cv = cv_ref[...] + bdw_ref[...]
    mu = jnp.mean(cv, axis=-1, keepdims=True)
    cen = cv - mu
    var = jnp.mean(cen * cen, axis=-1, keepdims=True)
    yn = cen * lax.rsqrt(var + EPS) * gln_ref[...] + bln_ref[...]
    act = (yn * _sigmoid(yn)).astype(BF16)
    y = jnp.dot(act, wco_ref[...], preferred_element_type=F32) + bco_ref[...]
    yc_ref[0] = (gc_ref[...].astype(F32) * y).astype(BF16)


def _in_proj_call(x, mod3, g_mix, wglu, bglu, wq, wk, wv, wg, wlr, wa2, ba2, wgate, wdw, bdw, gln, bln, wco, bco, tm):
    B, S, D = x.shape
    tok = lambda n: pl.BlockSpec((1, tm, n), lambda b, s: (b, s, 0))
    out_dims = (GLA_DK, GLA_DK, GLA_DV, GLA_DV, GLA_DK, D_MODEL, D_MODEL)
    out_dtypes = (BF16, BF16, BF16, BF16, F32, BF16, BF16)
    consts = (g_mix, wglu, bglu, wq, wk, wv, wg, wlr, wa2, ba2, wgate, wdw, bdw, gln, bln, wco, bco)
    return pl.pallas_call(
        _in_proj_kernel,
        grid=(B, S // tm),
        in_specs=[tok(D), pl.BlockSpec((1, 6, D), lambda b, s: (b, 0, 0))] + [_const_spec(a.shape) for a in consts],
        out_specs=[tok(n) for n in out_dims],
        out_shape=[jax.ShapeDtypeStruct((B, S, n), dt) for n, dt in zip(out_dims, out_dtypes)],
        scratch_shapes=[pltpu.VMEM((CONV_DIM // LANES, tm + CONV_HALO, LANES), F32),
                        pltpu.VMEM((tm, CONV_DIM), F32),
                        pltpu.VMEM((tm, D_MODEL), BF16), pltpu.VMEM((tm, D_MODEL), BF16)],
        compiler_params=pltpu.CompilerParams(dimension_semantics=("arbitrary", "arbitrary"),
                                             vmem_limit_bytes=VMEM_LIMIT),
        name="in_proj",
    )(x, mod3, *consts)


def _split3(x):
    hi = x.astype(BF16)
    r1 = x - hi.astype(F32)
    mid = r1.astype(BF16)
    lo = (r1 - mid.astype(F32)).astype(BF16)
    return hi, mid, lo


def _gla_kernel(q_ref, k_ref, v_ref, la_ref, sg_ref, gg_ref, yc_ref, sel_ref, ggla_ref, wout_ref, m_ref,
                state_ref, b_ref, on_ref):
    tg = q_ref.shape[1]
    n_chunks = tg // CHUNK
    s = pl.program_id(1)

    @pl.when(s == 0)
    def _():
        state_ref[...] = jnp.zeros_like(state_ref)

    b_ref[...] = sum(jnp.dot(sel_ref[...], p, preferred_element_type=F32) for p in _split3(la_ref[0]))

    kdec, qb, qm, km, dec = [], [], [], [], []
    for c in range(n_chunks):
        rows = slice(c * CHUNK, (c + 1) * CHUNK)
        b = b_ref[rows, :]
        b_end = b_ref[(c + 1) * CHUNK - 1:(c + 1) * CHUNK, :]
        b_mid = b_ref[c * CHUNK + CHUNK // 2:c * CHUNK + CHUNK // 2 + 1, :]
        q = q_ref[0, rows, :].astype(F32) * (HEAD_K ** -0.5)
        k = k_ref[0, rows, :].astype(F32)
        kdec.append((k * jnp.exp(b_end - b)).astype(BF16))
        qb.append((q * jnp.exp(b)).astype(BF16))
        qm.append((q * jnp.exp(b - b_mid)).astype(BF16))
        km.append((k * jnp.exp(b_mid - b)).astype(BF16))
        dec.append(jnp.exp(b_end))
    qm = jnp.concatenate(qm, axis=0)
    km = jnp.concatenate(km, axis=0)

    ri = lax.broadcasted_iota(jnp.int32, (tg, tg), 0)
    ci = lax.broadcasted_iota(jnp.int32, (tg, tg), 1)
    keep = ((ri // CHUNK) == (ci // CHUNK)) & (ci <= ri)
    nt = (((1,), (1,)), ((), ()))
    tn = (((0,), (0,)), ((), ()))

    for h in range(GLA_HEADS):
        kc = slice(h * HEAD_K, (h + 1) * HEAD_K)
        vc = slice(h * HEAD_V, (h + 1) * HEAD_V)
        vh = v_ref[0, :, vc]
        scores = lax.dot_general(qm[:, kc], km[:, kc], nt, preferred_element_type=F32)
        o = jnp.dot(jnp.where(keep, scores, 0.0).astype(BF16), vh, preferred_element_type=F32)
        st = state_ref[h]
        inter = []
        for c in range(n_chunks):
            rows = slice(c * CHUNK, (c + 1) * CHUNK)
            inter.append(lax.dot_general(qb[c][:, kc], st.astype(BF16), nt, preferred_element_type=F32))
            kv = lax.dot_general(vh[rows, :], kdec[c][:, kc], tn, preferred_element_type=F32)
            st = st * dec[c][:, kc] + kv
        state_ref[h] = st
        o = o + jnp.concatenate(inter, axis=0)
        on = _rms(o) * ggla_ref[:, vc] * sg_ref[0, :, vc].astype(F32)
        on_ref[:, vc] = on.astype(BF16)

    y = jnp.dot(on_ref[...], wout_ref[...], preferred_element_type=F32)
    m_ref[0] = (yc_ref[0].astype(F32) + gg_ref[0].astype(F32) * y).astype(BF16)


def _chunk_selectors(tg):
    row = np.arange(tg)[:, None]
    col = np.arange(tg)[None, :]
    return jnp.asarray(((row // CHUNK) == (col // CHUNK)) & (col <= row), dtype=BF16)


def _gla_call(q, k, v, la, sg, gg, yc, ggla, wout, tg):
    B, S, _ = q.shape
    tok = lambda n: pl.BlockSpec((1, tg, n), lambda b, s: (b, s, 0))
    consts = (_chunk_selectors(tg), ggla, wout)
    return pl.pallas_call(
        _gla_kernel,
        grid=(B, S // tg),
        in_specs=[tok(GLA_DK), tok(GLA_DK), tok(GLA_DV), tok(GLA_DK), tok(GLA_DV), tok(D_MODEL), tok(D_MODEL)]
                 + [_const_spec(a.shape) for a in consts],
        out_specs=tok(D_MODEL),
        out_shape=jax.ShapeDtypeStruct((B, S, D_MODEL), BF16),
        scratch_shapes=[pltpu.VMEM((GLA_HEADS, HEAD_V, HEAD_K), F32),
                        pltpu.VMEM((tg, GLA_DK), F32), pltpu.VMEM((tg, GLA_DV), BF16)],
        compiler_params=pltpu.CompilerParams(dimension_semantics=("arbitrary", "arbitrary"),
                                             vmem_limit_bytes=VMEM_LIMIT),
        name="gla",
    )(q, k, v, la, sg, gg, yc, *consts)


def _mlp_kernel(x_ref, m_ref, mod_ref, wo_ref, gmlp_ref, w1_ref, w2_ref, gfin_ref, o_ref):
    gt1 = mod_ref[0, 2:3, :]
    sh2 = mod_ref[0, 3:4, :]
    sc2 = mod_ref[0, 4:5, :]
    gt2 = mod_ref[0, 5:6, :]
    x1 = x_ref[0] + gt1 * jnp.dot(m_ref[0], wo_ref[...], preferred_element_type=F32)
    h2 = ((_rms(x1) * gmlp_ref[...]) * (1.0 + sc2) + sh2).astype(BF16)
    nb = 1024
    f = jnp.zeros_like(x1)
    for j in range(D_FF // nb):
        t = jnp.dot(h2, w1_ref[:, j * nb:(j + 1) * nb], preferred_element_type=F32)
        t = jnp.maximum(t, 0.0)
        f = f + jnp.dot((t * t).astype(BF16), w2_ref[j * nb:(j + 1) * nb, :], preferred_element_type=F32)
    x2 = x1 + gt2 * f
    o_ref[0] = _rms(x2) * gfin_ref[...]


def _mlp_call(x, m, mod3, wo, gmlp, w1, w2, gfin, tm):
    B, S, D = x.shape
    tok = pl.BlockSpec((1, tm, D), lambda b, s: (b, s, 0))
    consts_a = (wo, gmlp, w1, w2, gfin)
    return pl.pallas_call(
        _mlp_kernel,
        grid=(B, S // tm),
        in_specs=[tok, tok, pl.BlockSpec((1, 6, D), lambda b, s: (b, 0, 0))] + [_const_spec(a.shape) for a in consts_a],
        out_specs=tok,
        out_shape=jax.ShapeDtypeStruct((B, S, D), F32),
        compiler_params=pltpu.CompilerParams(dimension_semantics=("arbitrary", "arbitrary"),
                                             vmem_limit_bytes=VMEM_LIMIT),
        name="mlp",
    )(x, m, mod3, *consts_a)


def _tile(S, want):
    t = min(S, want)
    assert S % t == 0 and t % CHUNK == 0, (S, t)
    return t


def kernel(x, c, w_ada, b_ada, g_mix, w_in, b_glu, w_dw, b_dw, g_cln, b_cln, w_conv_out, b_conv_out,
           w_a2, b_a2, g_gla, w_gla_out, w_o, g_mlp, w_ff1, w_ff2, g_final):
    B, S, D = x.shape
    assert D == D_MODEL and w_ada.shape[0] == 1
    row = lambda a: a.reshape(1, -1)

    mod3 = _mod_call(c, w_ada[0], b_ada[0]).reshape(B, 6, D)

    wi = w_in[0].astype(BF16)
    o = 0
    wglu = wi[:, o:o + 2 * CONV_DIM]; o += 2 * CONV_DIM
    wq = wi[:, o:o + GLA_DK]; o += GLA_DK
    wk = wi[:, o:o + GLA_DK]; o += GLA_DK
    wv = wi[:, o:o + GLA_DV]; o += GLA_DV
    wg = wi[:, o:o + GLA_DV]; o += GLA_DV
    wlr = jnp.pad(wi[:, o:o + GATE_RANK], ((0, 0), (0, LANES - GATE_RANK))); o += GATE_RANK
    wgate = wi[:, o:o + 2 * D_MODEL]
    wa2 = jnp.pad(w_a2[0].astype(BF16), ((0, LANES - GATE_RANK), (0, 0)))

    q, k, v, sg, la, gg, yc = _in_proj_call(
        x, mod3, row(g_mix[0]), wglu, row(b_glu[0]), wq, wk, wv, wg, wlr, wa2, row(b_a2[0]), wgate,
        w_dw[0], row(b_dw[0]), row(g_cln[0]), row(b_cln[0]), w_conv_out[0].astype(BF16), row(b_conv_out[0]),
        tm=_tile(S, 512))
    m = _gla_call(q, k, v, la, sg, gg, yc, row(g_gla[0]), w_gla_out[0].astype(BF16), tg=_tile(S, 256))
    return _mlp_call(x, m, mod3, w_o[0].astype(BF16), row(g_mlp[0]), w_ff1[0].astype(BF16),
                     w_ff2[0].astype(BF16), row(g_final), tm=_tile(S, 512))
```

```python
import functools

import jax
import jax.numpy as jnp
import numpy as np
from jax import lax
from jax.experimental import pallas as pl
from jax.experimental.pallas import tpu as pltpu

D_MODEL = 1024
CONV_DIM = 1024
CONV_WIDTH = 31
GLA_HEADS = 4
GLA_DK = 512
GLA_DV = 1024
HEAD_K = GLA_DK // GLA_HEADS
HEAD_V = GLA_DV // GLA_HEADS
GATE_RANK = 16
GATE_TAU = 16.0
CHUNK = 64
D_FF = 4 * D_MODEL
EPS = 1e-6

LANES = 128
SUBLANES = 8
CONV_HALO = 32
CONV_ROWS = 64
VMEM_LIMIT = 56 * 1024 * 1024

F32 = jnp.float32
BF16 = jnp.bfloat16


def _const_spec(shape):
    nd = len(shape)
    return pl.BlockSpec(shape, lambda *_: (0,) * nd, pipeline_mode=pl.Buffered(1))


def _rms(x):
    return x * lax.rsqrt(jnp.mean(x * x, axis=-1, keepdims=True) + EPS)


def _sigmoid(x):
    return 1.0 / (1.0 + jnp.exp(-x))


def _mod_kernel(c_ref, w_ref, b_ref, o_ref):
    c = c_ref[...]
    s = c * _sigmoid(c)
    o_ref[...] = jnp.dot(s, w_ref[...], precision=lax.Precision.HIGHEST,
                         preferred_element_type=F32) + b_ref[...]


def _mod_call(c, w_ada, b_ada):
    B = c.shape[0]
    n = w_ada.shape[1]
    bn = 1024
    return pl.pallas_call(
        _mod_kernel,
        grid=(n // bn,),
        in_specs=[pl.BlockSpec((B, D_MODEL), lambda j: (0, 0)),
                  pl.BlockSpec((D_MODEL, bn), lambda j: (0, j)),
                  pl.BlockSpec((1, bn), lambda j: (0, j))],
        out_specs=pl.BlockSpec((B, bn), lambda j: (0, j)),
        out_shape=jax.ShapeDtypeStruct((B, n), F32),
        name="mod",
    )(c, w_ada, b_ada.reshape(1, n))


def _conv_block(ext_ref, wdw_ref, cv_ref, base, lb):
    first = CONV_HALO - (CONV_WIDTH - 1)
    lanes = slice(lb * LANES, (lb + 1) * LANES)
    acc = jnp.zeros((CONV_ROWS, LANES), F32)
    for k in range(CONV_WIDTH):
        acc = acc + wdw_ref[k:k + 1, lanes] * ext_ref[lb, base + first + k:base + first + k + CONV_ROWS, :]
    cv_ref[base:base + CONV_ROWS, lanes] = acc


def _in_proj_kernel(x_ref, mod_ref, gmix_ref, wglu_ref, bglu_ref, wq_ref, wk_ref, wv_ref, wg_ref,
                    wlr_ref, wa2_ref, ba2_ref, wgate_ref, wdw_ref, bdw_ref, gln_ref, bln_ref, wco_ref, bco_ref,
                    q_ref, k_ref, v_ref, sg_ref, la_ref, gg_ref, yc_ref, ext_ref, cv_ref, hb_ref, gc_ref):
    tm = x_ref.shape[1]
    s = pl.program_id(1)
    n_lb = CONV_DIM // LANES

    @pl.when(s == 0)
    def _():
        ext_ref[:, 0:CONV_HALO, :] = jnp.zeros((n_lb, CONV_HALO, LANES), F32)

    @pl.when(s != 0)
    def _():
        ext_ref[:, 0:CONV_HALO, :] = ext_ref[:, tm:tm + CONV_HALO, :]

    x = x_ref[0]
    sh1 = mod_ref[0, 0:1, :]
    sc1 = mod_ref[0, 1:2, :]
    h = (_rms(x) * gmix_ref[...]) * (1.0 + sc1) + sh1
    hb_ref[...] = h.astype(BF16)

    def proj(w_ref, lo, n):
        return jnp.dot(hb_ref[...], w_ref[:, lo:lo + n], preferred_element_type=F32)

    nb = 512
    for j in range(CONV_DIM // nb):
        lo = j * nb
        a = proj(wglu_ref, lo, nb) + bglu_ref[:, lo:lo + nb]
        b = proj(wglu_ref, CONV_DIM + lo, nb) + bglu_ref[:, CONV_DIM + lo:CONV_DIM + lo + nb]
        u = a * _sigmoid(b)
        for i in range(nb // LANES):
            ext_ref[lo // LANES + i, CONV_HALO:CONV_HALO + tm, :] = u[:, i * LANES:(i + 1) * LANES]

    def q_seg():
        q_ref[0] = proj(wq_ref, 0, GLA_DK).astype(BF16)

    def k_seg():
        k_ref[0] = proj(wk_ref, 0, GLA_DK).astype(BF16)

    def v_seg(lo):
        v_ref[0, :, lo:lo + nb] = proj(wv_ref, lo, nb).astype(BF16)

    def g_seg(lo):
        g = proj(wg_ref, lo, nb)
        sg_ref[0, :, lo:lo + nb] = (g * _sigmoid(g)).astype(BF16)

    def la_seg():
        p_lr = proj(wlr_ref, 0, LANES).astype(BF16)
        z = jnp.dot(p_lr, wa2_ref[...], preferred_element_type=F32) + ba2_ref[...]
        la_ref[0] = (jnp.minimum(z, 0.0) - jnp.log(1.0 + jnp.exp(-jnp.abs(z)))) * (1.0 / GATE_TAU)

    def gg_seg(lo):
        gg_ref[0, :, lo:lo + nb] = _sigmoid(proj(wgate_ref, D_MODEL + lo, nb)).astype(BF16)

    def gc_seg(lo):
        gc_ref[:, lo:lo + nb] = _sigmoid(proj(wgate_ref, lo, nb)).astype(BF16)

    segments = [q_seg, k_seg, functools.partial(v_seg, 0), functools.partial(v_seg, nb),
                functools.partial(g_seg, 0), functools.partial(g_seg, nb), la_seg,
                functools.partial(gg_seg, 0), functools.partial(gg_seg, nb),
                functools.partial(gc_seg, 0), functools.partial(gc_seg, nb)]
    conv_blocks = [(r * CONV_ROWS, lb) for r in range(tm // CONV_ROWS) for lb in range(n_lb)]
    bounds = [round(i * len(conv_blocks) / len(segments)) for i in range(len(segments) + 1)]
    for i, seg in enumerate(segments):
        seg()
        for base, lb in conv_blocks[bounds[i]:bounds[i + 1]]:
            _conv_block(ext_ref, wdw_ref, cv_ref, base, lb)

    cv = cv_ref[...] + bdw_ref[...]
    mu = jnp.mean(cv, axis=-1, keepdims=True)
    cen = cv - mu
    var = jnp.mean(cen * cen, axis=-1, keepdims=True)
    yn = cen * lax.rsqrt(var + EPS) * gln_ref[...] + bln_ref[...]
    act = (yn * _sigmoid(yn)).astype(BF16)
    y = jnp.dot(act, wco_ref[...], preferred_element_type=F32) + bco_ref[...]
    yc_ref[0] = (gc_ref[...].astype(F32) * y).astype(BF16)


def _in_proj_call(x, mod3, g_mix, wglu, bglu, wq, wk, wv, wg, wlr, wa2, ba2, wgate, wdw, bdw, gln, bln, wco, bco, tm):
    B, S, D = x.shape
    tok = lambda n: pl.BlockSpec((1, tm, n), lambda b, s: (b, s, 0))
    out_dims = (GLA_DK, GLA_DK, GLA_DV, GLA_DV, GLA_DK, D_MODEL, D_MODEL)
    out_dtypes = (BF16, BF16, BF16, BF16, F32, BF16, BF16)
    consts = (g_mix, wglu, bglu, wq, wk, wv, wg, wlr, wa2, ba2, wgate, wdw, bdw, gln, bln, wco, bco)
    return pl.pallas_call(
        _in_proj_kernel,
        grid=(B, S // tm),
        in_specs=[tok(D), pl.BlockSpec((1, 6, D), lambda b, s: (b, 0, 0))] + [_const_spec(a.shape) for a in consts],
        out_specs=[tok(n) for n in out_dims],
        out_shape=[jax.ShapeDtypeStruct((B, S, n), dt) for n, dt in zip(out_dims, out_dtypes)],
        scratch_shapes=[pltpu.VMEM((CONV_DIM // LANES, tm + CONV_HALO, LANES), F32),
                        pltpu.VMEM((tm, CONV_DIM), F32),
                        pltpu.VMEM((tm, D_MODEL), BF16), pltpu.VMEM((tm, D_MODEL), BF16)],
        compiler_params=pltpu.CompilerParams(dimension_semantics=("arbitrary", "arbitrary"),
                                             vmem_limit_bytes=VMEM_LIMIT),
        name="in_proj",
    )(x, mod3, *consts)


def _split3(x):
    hi = x.astype(BF16)
    r1 = x - hi.astype(F32)
    mid = r1.astype(BF16)
    lo = (r1 - mid.astype(F32)).astype(BF16)
    return hi, mid, lo


def _gla_tile(bi, q_ref, k_ref, v_ref, la_ref, sg_ref, gg_ref, yc_ref, sel_ref, ggla_ref, wout_ref, m_ref,
              state_ref, b_ref, on_ref):
    tg = q_ref.shape[1]
    n_chunks = tg // CHUNK

    b_ref[bi] = sum(jnp.dot(sel_ref[...], p, preferred_element_type=F32) for p in _split3(la_ref[bi]))

    kdec, qb, qm, km, dec = [], [], [], [], []
    for c in range(n_chunks):
        rows = slice(c * CHUNK, (c + 1) * CHUNK)
        b = b_ref[bi, rows, :]
        b_end = b_ref[bi, (c + 1) * CHUNK - 1:(c + 1) * CHUNK, :]
        b_mid = b_ref[bi, c * CHUNK + CHUNK // 2:c * CHUNK + CHUNK // 2 + 1, :]
        q = q_ref[bi, rows, :].astype(F32) * (HEAD_K ** -0.5)
        k = k_ref[bi, rows, :].astype(F32)
        kdec.append((k * jnp.exp(b_end - b)).astype(BF16))
        qb.append((q * jnp.exp(b)).astype(BF16))
        qm.append((q * jnp.exp(b - b_mid)).astype(BF16))
        km.append((k * jnp.exp(b_mid - b)).astype(BF16))
        dec.append(jnp.exp(b_end))
    qm = jnp.concatenate(qm, axis=0)
    km = jnp.concatenate(km, axis=0)

    ri = lax.broadcasted_iota(jnp.int32, (tg, tg), 0)
    ci = lax.broadcasted_iota(jnp.int32, (tg, tg), 1)
    keep = ((ri // CHUNK) == (ci // CHUNK)) & (ci <= ri)
    nt = (((1,), (1,)), ((), ()))
    tn = (((0,), (0,)), ((), ()))

    for h in range(GLA_HEADS):
        kc = slice(h * HEAD_K, (h + 1) * HEAD_K)
        vc = slice(h * HEAD_V, (h + 1) * HEAD_V)
        vh = v_ref[bi, :, vc]
        scores = lax.dot_general(qm[:, kc], km[:, kc], nt, preferred_element_type=F32)
        o = jnp.dot(jnp.where(keep, scores, 0.0).astype(BF16), vh, preferred_element_type=F32)
        st = state_ref[bi, h]
        inter = []
        for c in range(n_chunks):
            rows = slice(c * CHUNK, (c + 1) * CHUNK)
            inter.append(lax.dot_general(qb[c][:, kc], st.astype(BF16), nt, preferred_element_type=F32))
            kv = lax.dot_general(vh[rows, :], kdec[c][:, kc], tn, preferred_element_type=F32)
            st = st * dec[c][:, kc] + kv
        state_ref[bi, h] = st
        o = o + jnp.concatenate(inter, axis=0)
        on = _rms(o) * ggla_ref[:, vc] * sg_ref[bi, :, vc].astype(F32)
        on_ref[bi, :, vc] = on.astype(BF16)

    y = jnp.dot(on_ref[bi], wout_ref[...], preferred_element_type=F32)
    m_ref[bi] = (yc_ref[bi].astype(F32) + gg_ref[bi].astype(F32) * y).astype(BF16)


def _gla_kernel(q_ref, k_ref, v_ref, la_ref, sg_ref, gg_ref, yc_ref, sel_ref, ggla_ref, wout_ref, m_ref,
                state_ref, b_ref, on_ref):
    @pl.when(pl.program_id(1) == 0)
    def _():
        state_ref[...] = jnp.zeros_like(state_ref)

    for bi in range(q_ref.shape[0]):
        _gla_tile(bi, q_ref, k_ref, v_ref, la_ref, sg_ref, gg_ref, yc_ref, sel_ref, ggla_ref, wout_ref, m_ref,
                  state_ref, b_ref, on_ref)


def _chunk_selectors(tg):
    row = np.arange(tg)[:, None]
    col = np.arange(tg)[None, :]
    return jnp.asarray(((row // CHUNK) == (col // CHUNK)) & (col <= row), dtype=BF16)


def _gla_call(q, k, v, la, sg, gg, yc, ggla, wout, tg, nbatch):
    B, S, _ = q.shape
    tok = lambda n: pl.BlockSpec((nbatch, tg, n), lambda b, s: (b, s, 0))
    consts = (_chunk_selectors(tg), ggla, wout)
    return pl.pallas_call(
        _gla_kernel,
        grid=(B // nbatch, S // tg),
        in_specs=[tok(GLA_DK), tok(GLA_DK), tok(GLA_DV), tok(GLA_DK), tok(GLA_DV), tok(D_MODEL), tok(D_MODEL)]
                 + [_const_spec(a.shape) for a in consts],
        out_specs=tok(D_MODEL),
        out_shape=jax.ShapeDtypeStruct((B, S, D_MODEL), BF16),
        scratch_shapes=[pltpu.VMEM((nbatch, GLA_HEADS, HEAD_V, HEAD_K), F32),
                        pltpu.VMEM((nbatch, tg, GLA_DK), F32), pltpu.VMEM((nbatch, tg, GLA_DV), BF16)],
        compiler_params=pltpu.CompilerParams(dimension_semantics=("arbitrary", "arbitrary"),
                                             vmem_limit_bytes=VMEM_LIMIT),
        name="gla",
    )(q, k, v, la, sg, gg, yc, *consts)


def _mlp_kernel(x_ref, m_ref, mod_ref, wo_ref, gmlp_ref, w1_ref, w2_ref, gfin_ref, o_ref):
    gt1 = mod_ref[0, 2:3, :]
    sh2 = mod_ref[0, 3:4, :]
    sc2 = mod_ref[0, 4:5, :]
    gt2 = mod_ref[0, 5:6, :]
    x1 = x_ref[0] + gt1 * jnp.dot(m_ref[0], wo_ref[...], preferred_element_type=F32)
    h2 = ((_rms(x1) * gmlp_ref[...]) * (1.0 + sc2) + sh2).astype(BF16)
    nb = 1024
    f = jnp.zeros_like(x1)
    for j in range(D_FF // nb):
        t = jnp.dot(h2, w1_ref[:, j * nb:(j + 1) * nb], preferred_element_type=F32)
        t = jnp.maximum(t, 0.0)
        f = f + jnp.dot((t * t).astype(BF16), w2_ref[j * nb:(j + 1) * nb, :], preferred_element_type=F32)
    x2 = x1 + gt2 * f
    o_ref[0] = _rms(x2) * gfin_ref[...]


def _mlp_call(x, m, mod3, wo, gmlp, w1, w2, gfin, tm):
    B, S, D = x.shape
    tok = pl.BlockSpec((1, tm, D), lambda b, s: (b, s, 0))
    consts_a = (wo, gmlp, w1, w2, gfin)
    return pl.pallas_call(
        _mlp_kernel,
        grid=(B, S // tm),
        in_specs=[tok, tok, pl.BlockSpec((1, 6, D), lambda b, s: (b, 0, 0))] + [_const_spec(a.shape) for a in consts_a],
        out_specs=tok,
        out_shape=jax.ShapeDtypeStruct((B, S, D), F32),
        compiler_params=pltpu.CompilerParams(dimension_semantics=("arbitrary", "arbitrary"),
                                             vmem_limit_bytes=VMEM_LIMIT),
        name="mlp",
    )(x, m, mod3, *consts_a)


def _tile(S, want):
    t = min(S, want)
    assert S % t == 0 and t % CHUNK == 0, (S, t)
    return t


def kernel(x, c, w_ada, b_ada, g_mix, w_in, b_glu, w_dw, b_dw, g_cln, b_cln, w_conv_out, b_conv_out,
           w_a2, b_a2, g_gla, w_gla_out, w_o, g_mlp, w_ff1, w_ff2, g_final):
    B, S, D = x.shape
    assert D == D_MODEL and w_ada.shape[0] == 1
    row = lambda a: a.reshape(1, -1)

    mod3 = _mod_call(c, w_ada[0], b_ada[0]).reshape(B, 6, D)

    bounds = np.cumsum([0, 2 * CONV_DIM, GLA_DK, GLA_DK, GLA_DV, GLA_DV, GATE_RANK, 2 * D_MODEL])
    wglu, wq, wk, wv, wg, wlr, wgate = [w_in[0][:, lo:hi].astype(BF16) for lo, hi in zip(bounds[:-1], bounds[1:])]
    wlr = jnp.pad(wlr, ((0, 0), (0, LANES - GATE_RANK)))
    wa2 = jnp.pad(w_a2[0].astype(BF16), ((0, LANES - GATE_RANK), (0, 0)))

    q, k, v, sg, la, gg, yc = _in_proj_call(
        x, mod3, row(g_mix[0]), wglu, row(b_glu[0]), wq, wk, wv, wg, wlr, wa2, row(b_a2[0]), wgate,
        w_dw[0], row(b_dw[0]), row(g_cln[0]), row(b_cln[0]), w_conv_out[0].astype(BF16), row(b_conv_out[0]),
        tm=_tile(S, 512))
    m = _gla_call(q, k, v, la, sg, gg, yc, row(g_gla[0]), w_gla_out[0].astype(BF16), tg=_tile(S, 256),
                  nbatch=2 if B % 2 == 0 else 1)
    return _mlp_call(x, m, mod3, w_o[0].astype(BF16), row(g_mlp[0]), w_ff1[0].astype(BF16),
                     w_ff2[0].astype(BF16), row(g_final), tm=_tile(S, 512))
```

```python
import functools

import jax
import jax.numpy as jnp
import numpy as np
from jax import lax
from jax.experimental import pallas as pl
from jax.experimental.pallas import tpu as pltpu

D_MODEL = 1024
CONV_DIM = 1024
CONV_WIDTH = 31
GLA_HEADS = 4
GLA_DK = 512
GLA_DV = 1024
HEAD_K = GLA_DK // GLA_HEADS
HEAD_V = GLA_DV // GLA_HEADS
GATE_RANK = 16
GATE_TAU = 16.0
CHUNK = 64
D_FF = 4 * D_MODEL
EPS = 1e-6

LANES = 128
SUBLANES = 8
CONV_HALO = 32
CONV_ROWS = 64
VMEM_LIMIT = 56 * 1024 * 1024

COL_GLU = 0
COL_Q = COL_GLU + 2 * CONV_DIM
COL_K = COL_Q + GLA_DK
COL_V = COL_K + GLA_DK
COL_G = COL_V + GLA_DV
COL_GATE = COL_G + GLA_DV
COL_LR = COL_GATE + 2 * D_MODEL

F32 = jnp.float32
BF16 = jnp.bfloat16


def _const_spec(shape):
    nd = len(shape)
    return pl.BlockSpec(shape, lambda *_: (0,) * nd, pipeline_mode=pl.Buffered(1))


def _rms(x):
    return x * lax.rsqrt(jnp.mean(x * x, axis=-1, keepdims=True) + EPS)


def _sigmoid(x):
    return 1.0 / (1.0 + jnp.exp(-x))


def _mod_kernel(c_ref, w_ref, b_ref, o_ref):
    c = c_ref[...]
    s = c * _sigmoid(c)
    o_ref[...] = jnp.dot(s, w_ref[...], precision=lax.Precision.HIGHEST,
                         preferred_element_type=F32) + b_ref[...]


def _mod_call(c, w_ada, b_ada):
    B = c.shape[0]
    n = w_ada.shape[1]
    bn = 1024
    return pl.pallas_call(
        _mod_kernel,
        grid=(n // bn,),
        in_specs=[pl.BlockSpec((B, D_MODEL), lambda j: (0, 0)),
                  pl.BlockSpec((D_MODEL, bn), lambda j: (0, j)),
                  pl.BlockSpec((1, bn), lambda j: (0, j))],
        out_specs=pl.BlockSpec((B, bn), lambda j: (0, j)),
        out_shape=jax.ShapeDtypeStruct((B, n), F32),
        name="mod",
    )(c, w_ada, b_ada.reshape(1, n))


def _conv_block(ext_ref, wdw_ref, cv_ref, base, lb):
    first = CONV_HALO - (CONV_WIDTH - 1)
    lanes = slice(lb * LANES, (lb + 1) * LANES)
    acc = jnp.zeros((CONV_ROWS, LANES), F32)
    for k in range(CONV_WIDTH):
        acc = acc + wdw_ref[k:k + 1, lanes] * ext_ref[lb, base + first + k:base + first + k + CONV_ROWS, :]
    cv_ref[base:base + CONV_ROWS, lanes] = acc


def _in_proj_kernel(x_ref, mod_ref, gmix_ref, win_ref, bglu_ref, wa2_ref, ba2_ref,
                    wdw_ref, bdw_ref, gln_ref, bln_ref, wco_ref, bco_ref,
                    q_ref, k_ref, v_ref, sg_ref, la_ref, gg_ref, yc_ref, ext_ref, cv_ref, hb_ref, gc_ref):
    tm = x_ref.shape[1]
    s = pl.program_id(1)
    n_lb = CONV_DIM // LANES

    @pl.when(s == 0)
    def _():
        ext_ref[:, 0:CONV_HALO, :] = jnp.zeros((n_lb, CONV_HALO, LANES), F32)

    @pl.when(s != 0)
    def _():
        ext_ref[:, 0:CONV_HALO, :] = ext_ref[:, tm:tm + CONV_HALO, :]

    x = x_ref[0]
    sh1 = mod_ref[0, 0:1, :]
    sc1 = mod_ref[0, 1:2, :]
    h = (_rms(x) * gmix_ref[...]) * (1.0 + sc1) + sh1
    hb_ref[...] = h.astype(BF16)

    def proj(lo, n):
        return jnp.dot(hb_ref[...], win_ref[:, lo:lo + n], preferred_element_type=F32)

    nb = 512
    for j in range(CONV_DIM // nb):
        lo = j * nb
        a = proj(COL_GLU + lo, nb) + bglu_ref[:, lo:lo + nb]
        b = proj(COL_GLU + CONV_DIM + lo, nb) + bglu_ref[:, CONV_DIM + lo:CONV_DIM + lo + nb]
        u = a * _sigmoid(b)
        for i in range(nb // LANES):
            ext_ref[lo // LANES + i, CONV_HALO:CONV_HALO + tm, :] = u[:, i * LANES:(i + 1) * LANES]

    def q_seg():
        q_ref[0] = proj(COL_Q, GLA_DK).astype(BF16)

    def k_seg():
        k_ref[0] = proj(COL_K, GLA_DK).astype(BF16)

    def v_seg(lo):
        v_ref[0, :, lo:lo + nb] = proj(COL_V + lo, nb).astype(BF16)

    def g_seg(lo):
        g = proj(COL_G + lo, nb)
        sg_ref[0, :, lo:lo + nb] = (g * _sigmoid(g)).astype(BF16)

    def la_seg():
        p_lr = proj(COL_LR, LANES).astype(BF16)
        z = jnp.dot(p_lr, wa2_ref[...], preferred_element_type=F32) + ba2_ref[...]
        la_ref[0] = (jnp.minimum(z, 0.0) - jnp.log(1.0 + jnp.exp(-jnp.abs(z)))) * (1.0 / GATE_TAU)

    def gg_seg(lo):
        gg_ref[0, :, lo:lo + nb] = _sigmoid(proj(COL_GATE + D_MODEL + lo, nb)).astype(BF16)

    def gc_seg(lo):
        gc_ref[:, lo:lo + nb] = _sigmoid(proj(COL_GATE + lo, nb)).astype(BF16)

    segments = [q_seg, k_seg, functools.partial(v_seg, 0), functools.partial(v_seg, nb),
                functools.partial(g_seg, 0), functools.partial(g_seg, nb), la_seg,
                functools.partial(gg_seg, 0), functools.partial(gg_seg, nb),
                functools.partial(gc_seg, 0), functools.partial(gc_seg, nb)]
    conv_blocks = [(r * CONV_ROWS, lb) for r in range(tm // CONV_ROWS) for lb in range(n_lb)]
    bounds = [round(i * len(conv_blocks) / len(segments)) for i in range(len(segments) + 1)]
    for i, seg in enumerate(segments):
        seg()
        for base, lb in conv_blocks[bounds[i]:bounds[i + 1]]:
            _conv_block(ext_ref, wdw_ref, cv_ref, base, lb)

    cv = cv_ref[...] + bdw_ref[...]
    mu = jnp.mean(cv, axis=-1, keepdims=True)
    cen = cv - mu
    var = jnp.mean(cen * cen, axis=-1, keepdims=True)
    yn = cen * lax.rsqrt(var + EPS) * gln_ref[...] + bln_ref[...]
    act = (yn * _sigmoid(yn)).astype(BF16)
    y = jnp.dot(act, wco_ref[...], preferred_element_type=F32) + bco_ref[...]
    yc_ref[0] = (gc_ref[...].astype(F32) * y).astype(BF16)


def _in_proj_call(x, mod3, g_mix, win, bglu, wa2, ba2, wdw, bdw, gln, bln, wco, bco, tm):
    B, S, D = x.shape
    tok = lambda n: pl.BlockSpec((1, tm, n), lambda b, s: (b, s, 0))
    out_dims = (GLA_DK, GLA_DK, GLA_DV, GLA_DV, GLA_DK, D_MODEL, D_MODEL)
    out_dtypes = (BF16, BF16, BF16, BF16, F32, BF16, BF16)
    consts = (g_mix, win, bglu, wa2, ba2, wdw, bdw, gln, bln, wco, bco)
    return pl.pallas_call(
        _in_proj_kernel,
        grid=(B, S // tm),
        in_specs=[tok(D), pl.BlockSpec((1, 6, D), lambda b, s: (b, 0, 0))] + [_const_spec(a.shape) for a in consts],
        out_specs=[tok(n) for n in out_dims],
        out_shape=[jax.ShapeDtypeStruct((B, S, n), dt) for n, dt in zip(out_dims, out_dtypes)],
        scratch_shapes=[pltpu.VMEM((CONV_DIM // LANES, tm + CONV_HALO, LANES), F32),
                        pltpu.VMEM((tm, CONV_DIM), F32),
                        pltpu.VMEM((tm, D_MODEL), BF16), pltpu.VMEM((tm, D_MODEL), BF16)],
        compiler_params=pltpu.CompilerParams(dimension_semantics=("arbitrary", "arbitrary"),
                                             vmem_limit_bytes=VMEM_LIMIT),
        name="in_proj",
    )(x, mod3, *consts)


def _split3(x):
    hi = x.astype(BF16)
    r1 = x - hi.astype(F32)
    mid = r1.astype(BF16)
    lo = (r1 - mid.astype(F32)).astype(BF16)
    return hi, mid, lo


def _gla_tile(bi, q_ref, k_ref, v_ref, la_ref, sg_ref, gg_ref, yc_ref, sel_ref, ggla_ref, wout_ref, m_ref,
              state_ref, b_ref, on_ref):
    tg = q_ref.shape[1]
    n_chunks = tg // CHUNK

    b_ref[bi] = sum(jnp.dot(sel_ref[...], p, preferred_element_type=F32) for p in _split3(la_ref[bi]))

    kdec, qb, qm, km, dec = [], [], [], [], []
    for c in range(n_chunks):
        rows = slice(c * CHUNK, (c + 1) * CHUNK)
        b = b_ref[bi, rows, :]
        b_end = b_ref[bi, (c + 1) * CHUNK - 1:(c + 1) * CHUNK, :]
        b_mid = b_ref[bi, c * CHUNK + CHUNK // 2:c * CHUNK + CHUNK // 2 + 1, :]
        q = q_ref[bi, rows, :].astype(F32) * (HEAD_K ** -0.5)
        k = k_ref[bi, rows, :].astype(F32)
        kdec.append((k * jnp.exp(b_end - b)).astype(BF16))
        qb.append((q * jnp.exp(b)).astype(BF16))
        qm.append((q * jnp.exp(b - b_mid)).astype(BF16))
        km.append((k * jnp.exp(b_mid - b)).astype(BF16))
        dec.append(jnp.exp(b_end))
    qm = jnp.concatenate(qm, axis=0)
    km = jnp.concatenate(km, axis=0)

    ri = lax.broadcasted_iota(jnp.int32, (tg, tg), 0)
    ci = lax.broadcasted_iota(jnp.int32, (tg, tg), 1)
    keep = ((ri // CHUNK) == (ci // CHUNK)) & (ci <= ri)
    nt = (((1,), (1,)), ((), ()))
    tn = (((0,), (0,)), ((), ()))

    for h in range(GLA_HEADS):
        kc = slice(h * HEAD_K, (h + 1) * HEAD_K)
        vc = slice(h * HEAD_V, (h + 1) * HEAD_V)
        vh = v_ref[bi, :, vc]
        scores = lax.dot_general(qm[:, kc], km[:, kc], nt, preferred_element_type=F32)
        o = jnp.dot(jnp.where(keep, scores, 0.0).astype(BF16), vh, preferred_element_type=F32)
        st = state_ref[bi, h]
        inter = []
        for c in range(n_chunks):
            rows = slice(c * CHUNK, (c + 1) * CHUNK)
            inter.append(lax.dot_general(qb[c][:, kc], st.astype(BF16), nt, preferred_element_type=F32))
            kv = lax.dot_general(vh[rows, :], kdec[c][:, kc], tn, preferred_element_type=F32)
            st = st * dec[c][:, kc] + kv
        state_ref[bi, h] = st
        o = o + jnp.concatenate(inter, axis=0)
        on = _rms(o) * ggla_ref[:, vc] * sg_ref[bi, :, vc].astype(F32)
        on_ref[bi, :, vc] = on.astype(BF16)

    y = jnp.dot(on_ref[bi], wout_ref[...], preferred_element_type=F32)
    m_ref[bi] = (yc_ref[bi].astype(F32) + gg_ref[bi].astype(F32) * y).astype(BF16)


def _gla_kernel(q_ref, k_ref, v_ref, la_ref, sg_ref, gg_ref, yc_ref, sel_ref, ggla_ref, wout_ref, m_ref,
                state_ref, b_ref, on_ref):
    @pl.when(pl.program_id(1) == 0)
    def _():
        state_ref[...] = jnp.zeros_like(state_ref)

    for bi in range(q_ref.shape[0]):
        _gla_tile(bi, q_ref, k_ref, v_ref, la_ref, sg_ref, gg_ref, yc_ref, sel_ref, ggla_ref, wout_ref, m_ref,
                  state_ref, b_ref, on_ref)


def _chunk_selectors(tg):
    row = np.arange(tg)[:, None]
    col = np.arange(tg)[None, :]
    return jnp.asarray(((row // CHUNK) == (col // CHUNK)) & (col <= row), dtype=BF16)


def _gla_call(q, k, v, la, sg, gg, yc, ggla, wout, tg, nbatch):
    B, S, _ = q.shape
    tok = lambda n: pl.BlockSpec((nbatch, tg, n), lambda b, s: (b, s, 0))
    consts = (_chunk_selectors(tg), ggla, wout)
    return pl.pallas_call(
        _gla_kernel,
        grid=(B // nbatch, S // tg),
        in_specs=[tok(GLA_DK), tok(GLA_DK), tok(GLA_DV), tok(GLA_DK), tok(GLA_DV), tok(D_MODEL), tok(D_MODEL)]
                 + [_const_spec(a.shape) for a in consts],
        out_specs=tok(D_MODEL),
        out_shape=jax.ShapeDtypeStruct((B, S, D_MODEL), BF16),
        scratch_shapes=[pltpu.VMEM((nbatch, GLA_HEADS, HEAD_V, HEAD_K), F32),
                        pltpu.VMEM((nbatch, tg, GLA_DK), F32), pltpu.VMEM((nbatch, tg, GLA_DV), BF16)],
        compiler_params=pltpu.CompilerParams(dimension_semantics=("arbitrary", "arbitrary"),
                                             vmem_limit_bytes=VMEM_LIMIT),
        name="gla",
    )(q, k, v, la, sg, gg, yc, *consts)


def _mlp_kernel(x_ref, m_ref, mod_ref, wo_ref, gmlp_ref, w1_ref, w2_ref, gfin_ref, o_ref):
    gt1 = mod_ref[0, 2:3, :]
    sh2 = mod_ref[0, 3:4, :]
    sc2 = mod_ref[0, 4:5, :]
    gt2 = mod_ref[0, 5:6, :]
    x1 = x_ref[0] + gt1 * jnp.dot(m_ref[0], wo_ref[...], preferred_element_type=F32)
    h2 = ((_rms(x1) * gmlp_ref[...]) * (1.0 + sc2) + sh2).astype(BF16)
    nb = 1024
    f = jnp.zeros_like(x1)
    for j in range(D_FF // nb):
        t = jnp.dot(h2, w1_ref[:, j * nb:(j + 1) * nb], preferred_element_type=F32)
        t = jnp.maximum(t, 0.0)
        f = f + jnp.dot((t * t).astype(BF16), w2_ref[j * nb:(j + 1) * nb, :], preferred_element_type=F32)
    x2 = x1 + gt2 * f
    o_ref[0] = _rms(x2) * gfin_ref[...]


def _mlp_call(x, m, mod3, wo, gmlp, w1, w2, gfin, tm):
    B, S, D = x.shape
    tok = pl.BlockSpec((1, tm, D), lambda b, s: (b, s, 0))
    consts_a = (wo, gmlp, w1, w2, gfin)
    return pl.pallas_call(
        _mlp_kernel,
        grid=(B, S // tm),
        in_specs=[tok, tok, pl.BlockSpec((1, 6, D), lambda b, s: (b, 0, 0))] + [_const_spec(a.shape) for a in consts_a],
        out_specs=tok,
        out_shape=jax.ShapeDtypeStruct((B, S, D), F32),
        compiler_params=pltpu.CompilerParams(dimension_semantics=("arbitrary", "arbitrary"),
                                             vmem_limit_bytes=VMEM_LIMIT),
        name="mlp",
    )(x, m, mod3, *consts_a)


PROJ_TOKENS = 512
GLA_TOKENS = 256
GLA_SEQS = (4, 2, 1)


def _tile(S, want):
    t = min(S, want)
    assert S % t == 0 and t % CHUNK == 0, (S, t)
    return t


def kernel(x, c, w_ada, b_ada, g_mix, w_in, b_glu, w_dw, b_dw, g_cln, b_cln, w_conv_out, b_conv_out,
           w_a2, b_a2, g_gla, w_gla_out, w_o, g_mlp, w_ff1, w_ff2, g_final):
    B, S, D = x.shape
    assert D == D_MODEL and w_ada.shape[0] == 1
    row = lambda a: a.reshape(1, -1)

    mod3 = _mod_call(c, w_ada[0], b_ada[0]).reshape(B, 6, D)

    lr_lo = COL_GATE
    lr_hi = lr_lo + GATE_RANK
    win = jnp.concatenate([w_in[0][:, :lr_lo], w_in[0][:, lr_hi:], w_in[0][:, lr_lo:lr_hi],
                           jnp.zeros((D, LANES - GATE_RANK), w_in.dtype)], axis=1).astype(BF16)
    wa2 = jnp.pad(w_a2[0].astype(BF16), ((0, LANES - GATE_RANK), (0, 0)))

    q, k, v, sg, la, gg, yc = _in_proj_call(
        x, mod3, row(g_mix[0]), win, row(b_glu[0]), wa2, row(b_a2[0]),
        w_dw[0], row(b_dw[0]), row(g_cln[0]), row(b_cln[0]), w_conv_out[0].astype(BF16), row(b_conv_out[0]),
        tm=_tile(S, PROJ_TOKENS))
    m = _gla_call(q, k, v, la, sg, gg, yc, row(g_gla[0]), w_gla_out[0].astype(BF16), tg=_tile(S, GLA_TOKENS),
                  nbatch=next(n for n in GLA_SEQS if B % n == 0))
    return _mlp_call(x, m, mod3, w_o[0].astype(BF16), row(g_mlp[0]), w_ff1[0].astype(BF16),
                     w_ff2[0].astype(BF16), row(g_final), tm=_tile(S, PROJ_TOKENS))
```

```python
import functools

import jax
import jax.numpy as jnp
import numpy as np
from jax import lax
from jax.experimental import pallas as pl
from jax.experimental.pallas import tpu as pltpu

D_MODEL = 1024
CONV_DIM = 1024
CONV_WIDTH = 31
GLA_HEADS = 4
GLA_DK = 512
GLA_DV = 1024
HEAD_K = GLA_DK // GLA_HEADS
HEAD_V = GLA_DV // GLA_HEADS
GATE_RANK = 16
GATE_TAU = 16.0
CHUNK = 64
D_FF = 4 * D_MODEL
EPS = 1e-6

LANES = 128
SUBLANES = 8
CONV_HALO = 32
CONV_ROWS = 64
VMEM_LIMIT = 56 * 1024 * 1024

COL_GLU = 0
COL_Q = COL_GLU + 2 * CONV_DIM
COL_K = COL_Q + GLA_DK
COL_V = COL_K + GLA_DK
COL_G = COL_V + GLA_DV
COL_GATE = COL_G + GLA_DV
COL_LR = COL_GATE + 2 * D_MODEL

F32 = jnp.float32
BF16 = jnp.bfloat16


def _const_spec(shape):
    nd = len(shape)
    return pl.BlockSpec(shape, lambda *_: (0,) * nd, pipeline_mode=pl.Buffered(1))


def _rms(x):
    return x * lax.rsqrt(jnp.mean(x * x, axis=-1, keepdims=True) + EPS)


def _sigmoid(x):
    return 1.0 / (1.0 + jnp.exp(-x))


def _mod_kernel(c_ref, w_ref, b_ref, o_ref):
    c = c_ref[...]
    s = c * _sigmoid(c)
    o_ref[...] = jnp.dot(s, w_ref[...], precision=lax.Precision.HIGHEST,
                         preferred_element_type=F32) + b_ref[...]


def _mod_call(c, w_ada, b_ada):
    B = c.shape[0]
    n = w_ada.shape[1]
    bn = 1024
    return pl.pallas_call(
        _mod_kernel,
        grid=(n // bn,),
        in_specs=[pl.BlockSpec((B, D_MODEL), lambda j: (0, 0)),
                  pl.BlockSpec((D_MODEL, bn), lambda j: (0, j)),
                  pl.BlockSpec((1, bn), lambda j: (0, j))],
        out_specs=pl.BlockSpec((B, bn), lambda j: (0, j)),
        out_shape=jax.ShapeDtypeStruct((B, n), F32),
        name="mod",
    )(c, w_ada, b_ada.reshape(1, n))


def _regroup_kernel(w_ref, o_ref):
    w = w_ref[0]
    lr_lo = COL_GATE
    lr_hi = lr_lo + GATE_RANK
    o_ref[:, 0:lr_lo] = w[:, 0:lr_lo].astype(BF16)
    o_ref[:, lr_lo:COL_LR] = w[:, lr_hi:].astype(BF16)
    pad = jnp.zeros((w.shape[0], LANES - GATE_RANK), BF16)
    o_ref[:, COL_LR:] = jnp.concatenate([w[:, lr_lo:lr_hi].astype(BF16), pad], axis=1)


def _regroup_call(w_in):
    _, d, n = w_in.shape
    rows = 256
    return pl.pallas_call(
        _regroup_kernel,
        grid=(d // rows,),
        in_specs=[pl.BlockSpec((1, rows, n), lambda i: (0, i, 0))],
        out_specs=pl.BlockSpec((rows, COL_LR + LANES), lambda i: (i, 0)),
        out_shape=jax.ShapeDtypeStruct((d, COL_LR + LANES), BF16),
        compiler_params=pltpu.CompilerParams(dimension_semantics=("arbitrary",), vmem_limit_bytes=VMEM_LIMIT),
        name="regroup",
    )(w_in)


def _conv_block(ext_ref, wdw_ref, cv_ref, base, lb):
    first = CONV_HALO - (CONV_WIDTH - 1)
    lanes = slice(lb * LANES, (lb + 1) * LANES)
    acc = jnp.zeros((CONV_ROWS, LANES), F32)
    for k in range(CONV_WIDTH):
        acc = acc + wdw_ref[k:k + 1, lanes] * ext_ref[lb, base + first + k:base + first + k + CONV_ROWS, :]
    cv_ref[base:base + CONV_ROWS, lanes] = acc


def _in_proj_kernel(x_ref, mod_ref, gmix_ref, win_ref, bglu_ref, wa2_ref, ba2_ref,
                    wdw_ref, bdw_ref, gln_ref, bln_ref, wco_ref, bco_ref,
                    q_ref, k_ref, v_ref, sg_ref, la_ref, gg_ref, yc_ref, ext_ref, cv_ref, hb_ref, gc_ref):
    tm = x_ref.shape[1]
    s = pl.program_id(1)
    n_lb = CONV_DIM // LANES

    @pl.when(s == 0)
    def _():
        ext_ref[:, 0:CONV_HALO, :] = jnp.zeros((n_lb, CONV_HALO, LANES), F32)

    @pl.when(s != 0)
    def _():
        ext_ref[:, 0:CONV_HALO, :] = ext_ref[:, tm:tm + CONV_HALO, :]

    x = x_ref[0]
    sh1 = mod_ref[0, 0:1, :]
    sc1 = mod_ref[0, 1:2, :]
    h = (_rms(x) * gmix_ref[...]) * (1.0 + sc1) + sh1
    hb_ref[...] = h.astype(BF16)

    def proj(lo, n):
        return jnp.dot(hb_ref[...], win_ref[:, lo:lo + n], preferred_element_type=F32)

    nb = 512
    for j in range(CONV_DIM // nb):
        lo = j * nb
        a = proj(COL_GLU + lo, nb) + bglu_ref[:, lo:lo + nb]
        b = proj(COL_GLU + CONV_DIM + lo, nb) + bglu_ref[:, CONV_DIM + lo:CONV_DIM + lo + nb]
        u = a * _sigmoid(b)
        for i in range(nb // LANES):
            ext_ref[lo // LANES + i, CONV_HALO:CONV_HALO + tm, :] = u[:, i * LANES:(i + 1) * LANES]

    def q_seg():
        q_ref[0] = proj(COL_Q, GLA_DK).astype(BF16)

    def k_seg():
        k_ref[0] = proj(COL_K, GLA_DK).astype(BF16)

    def v_seg(lo):
        v_ref[0, :, lo:lo + nb] = proj(COL_V + lo, nb).astype(BF16)

    def g_seg(lo):
        g = proj(COL_G + lo, nb)
        sg_ref[0, :, lo:lo + nb] = (g * _sigmoid(g)).astype(BF16)

    def la_seg():
        p_lr = proj(COL_LR, LANES).astype(BF16)
        z = jnp.dot(p_lr, wa2_ref[...], preferred_element_type=F32) + ba2_ref[...]
        la_ref[0] = (jnp.minimum(z, 0.0) - jnp.log(1.0 + jnp.exp(-jnp.abs(z)))) * (1.0 / GATE_TAU)

    def gg_seg(lo):
        gg_ref[0, :, lo:lo + nb] = _sigmoid(proj(COL_GATE + D_MODEL + lo, nb)).astype(BF16)

    def gc_seg(lo):
        gc_ref[:, lo:lo + nb] = _sigmoid(proj(COL_GATE + lo, nb)).astype(BF16)

    segments = [q_seg, k_seg, functools.partial(v_seg, 0), functools.partial(v_seg, nb),
                functools.partial(g_seg, 0), functools.partial(g_seg, nb), la_seg,
                functools.partial(gg_seg, 0), functools.partial(gg_seg, nb),
                functools.partial(gc_seg, 0), functools.partial(gc_seg, nb)]
    conv_blocks = [(r * CONV_ROWS, lb) for r in range(tm // CONV_ROWS) for lb in range(n_lb)]
    bounds = [round(i * len(conv_blocks) / len(segments)) for i in range(len(segments) + 1)]
    for i, seg in enumerate(segments):
        seg()
        for base, lb in conv_blocks[bounds[i]:bounds[i + 1]]:
            _conv_block(ext_ref, wdw_ref, cv_ref, base, lb)

    cv = cv_ref[...] + bdw_ref[...]
    mu = jnp.mean(cv, axis=-1, keepdims=True)
    cen = cv - mu
    var = jnp.mean(cen * cen, axis=-1, keepdims=True)
    yn = cen * lax.rsqrt(var + EPS) * gln_ref[...] + bln_ref[...]
    act = (yn * _sigmoid(yn)).astype(BF16)
    y = jnp.dot(act, wco_ref[...], preferred_element_type=F32) + bco_ref[...]
    yc_ref[0] = (gc_ref[...].astype(F32) * y).astype(BF16)


def _in_proj_call(x, mod3, g_mix, win, bglu, wa2, ba2, wdw, bdw, gln, bln, wco, bco, tm):
    B, S, D = x.shape
    tok = lambda n: pl.BlockSpec((1, tm, n), lambda b, s: (b, s, 0))
    out_dims = (GLA_DK, GLA_DK, GLA_DV, GLA_DV, GLA_DK, D_MODEL, D_MODEL)
    out_dtypes = (BF16, BF16, BF16, BF16, F32, BF16, BF16)
    consts = (g_mix, win, bglu, wa2, ba2, wdw, bdw, gln, bln, wco, bco)
    return pl.pallas_call(
        _in_proj_kernel,
        grid=(B, S // tm),
        in_specs=[tok(D), pl.BlockSpec((1, 6, D), lambda b, s: (b, 0, 0))] + [_const_spec(a.shape) for a in consts],
        out_specs=[tok(n) for n in out_dims],
        out_shape=[jax.ShapeDtypeStruct((B, S, n), dt) for n, dt in zip(out_dims, out_dtypes)],
        scratch_shapes=[pltpu.VMEM((CONV_DIM // LANES, tm + CONV_HALO, LANES), F32),
                        pltpu.VMEM((tm, CONV_DIM), F32),
                        pltpu.VMEM((tm, D_MODEL), BF16), pltpu.VMEM((tm, D_MODEL), BF16)],
        compiler_params=pltpu.CompilerParams(dimension_semantics=("arbitrary", "arbitrary"),
                                             vmem_limit_bytes=VMEM_LIMIT),
        name="in_proj",
    )(x, mod3, *consts)


def _split3(x):
    hi = x.astype(BF16)
    r1 = x - hi.astype(F32)
    mid = r1.astype(BF16)
    lo = (r1 - mid.astype(F32)).astype(BF16)
    return hi, mid, lo


def _gla_tile(bi, q_ref, k_ref, v_ref, la_ref, sg_ref, gg_ref, yc_ref, sel_ref, ggla_ref, wout_ref, m_ref,
              state_ref, b_ref, on_ref):
    tg = q_ref.shape[1]
    n_chunks = tg // CHUNK

    b_ref[bi] = sum(jnp.dot(sel_ref[...], p, preferred_element_type=F32) for p in _split3(la_ref[bi]))

    kdec, qb, qm, km, dec = [], [], [], [], []
    for c in range(n_chunks):
        rows = slice(c * CHUNK, (c + 1) * CHUNK)
        b = b_ref[bi, rows, :]
        b_end = b_ref[bi, (c + 1) * CHUNK - 1:(c + 1) * CHUNK, :]
        b_mid = b_ref[bi, c * CHUNK + CHUNK // 2:c * CHUNK + CHUNK // 2 + 1, :]
        q = q_ref[bi, rows, :].astype(F32) * (HEAD_K ** -0.5)
        k = k_ref[bi, rows, :].astype(F32)
        kdec.append((k * jnp.exp(b_end - b)).astype(BF16))
        qb.append((q * jnp.exp(b)).astype(BF16))
        qm.append((q * jnp.exp(b - b_mid)).astype(BF16))
        km.append((k * jnp.exp(b_mid - b)).astype(BF16))
        dec.append(jnp.exp(b_end))
    qm = jnp.concatenate(qm, axis=0)
    km = jnp.concatenate(km, axis=0)

    ri = lax.broadcasted_iota(jnp.int32, (tg, tg), 0)
    ci = lax.broadcasted_iota(jnp.int32, (tg, tg), 1)
    keep = ((ri // CHUNK) == (ci // CHUNK)) & (ci <= ri)
    nt = (((1,), (1,)), ((), ()))
    tn = (((0,), (0,)), ((), ()))

    for h in range(GLA_HEADS):
        kc = slice(h * HEAD_K, (h + 1) * HEAD_K)
        vc = slice(h * HEAD_V, (h + 1) * HEAD_V)
        vh = v_ref[bi, :, vc]
        scores = lax.dot_general(qm[:, kc], km[:, kc], nt, preferred_element_type=F32)
        o = jnp.dot(jnp.where(keep, scores, 0.0).astype(BF16), vh, preferred_element_type=F32)
        st = state_ref[bi, h]
        inter = []
        for c in range(n_chunks):
            rows = slice(c * CHUNK, (c + 1) * CHUNK)
            inter.append(lax.dot_general(qb[c][:, kc], st.astype(BF16), nt, preferred_element_type=F32))
            kv = lax.dot_general(vh[rows, :], kdec[c][:, kc], tn, preferred_element_type=F32)
            st = st * dec[c][:, kc] + kv
        state_ref[bi, h] = st
        o = o + jnp.concatenate(inter, axis=0)
        on = _rms(o) * ggla_ref[:, vc] * sg_ref[bi, :, vc].astype(F32)
        on_ref[bi, :, vc] = on.astype(BF16)

    y = jnp.dot(on_ref[bi], wout_ref[...], preferred_element_type=F32)
    m_ref[bi] = (yc_ref[bi].astype(F32) + gg_ref[bi].astype(F32) * y).astype(BF16)


def _gla_kernel(q_ref, k_ref, v_ref, la_ref, sg_ref, gg_ref, yc_ref, sel_ref, ggla_ref, wout_ref, m_ref,
                state_ref, b_ref, on_ref):
    @pl.when(pl.program_id(1) == 0)
    def _():
        state_ref[...] = jnp.zeros_like(state_ref)

    for bi in range(q_ref.shape[0]):
        _gla_tile(bi, q_ref, k_ref, v_ref, la_ref, sg_ref, gg_ref, yc_ref, sel_ref, ggla_ref, wout_ref, m_ref,
                  state_ref, b_ref, on_ref)


def _chunk_selectors(tg):
    row = np.arange(tg)[:, None]
    col = np.arange(tg)[None, :]
    return jnp.asarray(((row // CHUNK) == (col // CHUNK)) & (col <= row), dtype=BF16)


def _gla_call(q, k, v, la, sg, gg, yc, ggla, wout, tg, nbatch):
    B, S, _ = q.shape
    tok = lambda n: pl.BlockSpec((nbatch, tg, n), lambda b, s: (b, s, 0))
    consts = (_chunk_selectors(tg), ggla, wout)
    return pl.pallas_call(
        _gla_kernel,
        grid=(B // nbatch, S // tg),
        in_specs=[tok(GLA_DK), tok(GLA_DK), tok(GLA_DV), tok(GLA_DK), tok(GLA_DV), tok(D_MODEL), tok(D_MODEL)]
                 + [_const_spec(a.shape) for a in consts],
        out_specs=tok(D_MODEL),
        out_shape=jax.ShapeDtypeStruct((B, S, D_MODEL), BF16),
        scratch_shapes=[pltpu.VMEM((nbatch, GLA_HEADS, HEAD_V, HEAD_K), F32),
                        pltpu.VMEM((nbatch, tg, GLA_DK), F32), pltpu.VMEM((nbatch, tg, GLA_DV), BF16)],
        compiler_params=pltpu.CompilerParams(dimension_semantics=("arbitrary", "arbitrary"),
                                             vmem_limit_bytes=VMEM_LIMIT),
        name="gla",
    )(q, k, v, la, sg, gg, yc, *consts)


def _mlp_kernel(x_ref, m_ref, mod_ref, wo_ref, gmlp_ref, w1_ref, w2_ref, gfin_ref, o_ref):
    gt1 = mod_ref[0, 2:3, :]
    sh2 = mod_ref[0, 3:4, :]
    sc2 = mod_ref[0, 4:5, :]
    gt2 = mod_ref[0, 5:6, :]
    x1 = x_ref[0] + gt1 * jnp.dot(m_ref[0], wo_ref[...], preferred_element_type=F32)
    h2 = ((_rms(x1) * gmlp_ref[...]) * (1.0 + sc2) + sh2).astype(BF16)
    nb = 1024
    f = jnp.zeros_like(x1)
    for j in range(D_FF // nb):
        t = jnp.dot(h2, w1_ref[:, j * nb:(j + 1) * nb], preferred_element_type=F32)
        t = jnp.maximum(t, 0.0)
        f = f + jnp.dot((t * t).astype(BF16), w2_ref[j * nb:(j + 1) * nb, :], preferred_element_type=F32)
    x2 = x1 + gt2 * f
    o_ref[0] = _rms(x2) * gfin_ref[...]


def _mlp_call(x, m, mod3, wo, gmlp, w1, w2, gfin, tm):
    B, S, D = x.shape
    tok = pl.BlockSpec((1, tm, D), lambda b, s: (b, s, 0))
    consts_a = (wo, gmlp, w1, w2, gfin)
    return pl.pallas_call(
        _mlp_kernel,
        grid=(B, S // tm),
        in_specs=[tok, tok, pl.BlockSpec((1, 6, D), lambda b, s: (b, 0, 0))] + [_const_spec(a.shape) for a in consts_a],
        out_specs=tok,
        out_shape=jax.ShapeDtypeStruct((B, S, D), F32),
        compiler_params=pltpu.CompilerParams(dimension_semantics=("arbitrary", "arbitrary"),
                                             vmem_limit_bytes=VMEM_LIMIT),
        name="mlp",
    )(x, m, mod3, *consts_a)


PROJ_TOKENS = 512
GLA_TOKENS = 256
GLA_SEQS = (4, 2, 1)


def _tile(S, want):
    t = min(S, want)
    assert S % t == 0 and t % CHUNK == 0, (S, t)
    return t


def kernel(x, c, w_ada, b_ada, g_mix, w_in, b_glu, w_dw, b_dw, g_cln, b_cln, w_conv_out, b_conv_out,
           w_a2, b_a2, g_gla, w_gla_out, w_o, g_mlp, w_ff1, w_ff2, g_final):
    B, S, D = x.shape
    assert D == D_MODEL and w_ada.shape[0] == 1
    row = lambda a: a.reshape(1, -1)

    mod3 = _mod_call(c, w_ada[0], b_ada[0]).reshape(B, 6, D)

    win = _regroup_call(w_in)
    wa2 = jnp.pad(w_a2[0].astype(BF16), ((0, LANES - GATE_RANK), (0, 0)))

    q, k, v, sg, la, gg, yc = _in_proj_call(
        x, mod3, row(g_mix[0]), win, row(b_glu[0]), wa2, row(b_a2[0]),
        w_dw[0], row(b_dw[0]), row(g_cln[0]), row(b_cln[0]), w_conv_out[0].astype(BF16), row(b_conv_out[0]),
        tm=_tile(S, PROJ_TOKENS))
    m = _gla_call(q, k, v, la, sg, gg, yc, row(g_gla[0]), w_gla_out[0].astype(BF16), tg=_tile(S, GLA_TOKENS),
                  nbatch=next(n for n in GLA_SEQS if B % n == 0))
    return _mlp_call(x, m, mod3, w_o[0].astype(BF16), row(g_mlp[0]), w_ff1[0].astype(BF16),
                     w_ff2[0].astype(BF16), row(g_final), tm=_tile(S, PROJ_TOKENS))
```

```python
import functools

import jax
import jax.numpy as jnp
import numpy as np
from jax import lax
from jax.experimental import pallas as pl
from jax.experimental.pallas import tpu as pltpu

D_MODEL = 1024
CONV_DIM = 1024
CONV_WIDTH = 31
GLA_HEADS = 4
GLA_DK = 512
GLA_DV = 1024
HEAD_K = GLA_DK // GLA_HEADS
HEAD_V = GLA_DV // GLA_HEADS
GATE_RANK = 16
GATE_TAU = 16.0
CHUNK = 64
D_FF = 4 * D_MODEL
EPS = 1e-6

LANES = 128
SUBLANES = 8
CONV_HALO = 32
CONV_ROWS = 64
VMEM_LIMIT = 56 * 1024 * 1024

COL_GLU = 0
COL_Q = COL_GLU + 2 * CONV_DIM
COL_K = COL_Q + GLA_DK
COL_V = COL_K + GLA_DK
COL_G = COL_V + GLA_DV
COL_GATE = COL_G + GLA_DV
COL_LR = COL_GATE + 2 * D_MODEL

F32 = jnp.float32
BF16 = jnp.bfloat16


def _const_spec(shape):
    nd = len(shape)
    return pl.BlockSpec(shape, lambda *_: (0,) * nd, pipeline_mode=pl.Buffered(1))


def _rms(x):
    return x * lax.rsqrt(jnp.mean(x * x, axis=-1, keepdims=True) + EPS)


def _sigmoid(x):
    return 1.0 / (1.0 + jnp.exp(-x))


def _mod_kernel(c_ref, w_ref, b_ref, o_ref):
    c = c_ref[...]
    s = c * _sigmoid(c)
    o_ref[...] = jnp.dot(s, w_ref[...], precision=lax.Precision.HIGHEST,
                         preferred_element_type=F32) + b_ref[...]


def _mod_call(c, w_ada, b_ada):
    B = c.shape[0]
    n = w_ada.shape[1]
    bn = 1024
    return pl.pallas_call(
        _mod_kernel,
        grid=(n // bn,),
        in_specs=[pl.BlockSpec((B, D_MODEL), lambda j: (0, 0)),
                  pl.BlockSpec((D_MODEL, bn), lambda j: (0, j)),
                  pl.BlockSpec((1, bn), lambda j: (0, j))],
        out_specs=pl.BlockSpec((B, bn), lambda j: (0, j)),
        out_shape=jax.ShapeDtypeStruct((B, n), F32),
        name="mod",
    )(c, w_ada, b_ada.reshape(1, n))


def _regroup_kernel(wt_ref, o_ref):
    lr_lo = COL_GATE
    lr_hi = lr_lo + GATE_RANK

    def put(col, blk):
        o_ref[:, col:col + LANES] = blk.T.astype(BF16)

    for col in range(0, lr_lo, LANES):
        put(col, wt_ref[0, col:col + LANES, :])
    for col in range(lr_lo, COL_LR, LANES):
        put(col, wt_ref[0, col + GATE_RANK:col + GATE_RANK + LANES, :])
    pad = jnp.zeros((LANES - GATE_RANK, wt_ref.shape[2]), F32)
    put(COL_LR, jnp.concatenate([wt_ref[0, lr_lo:lr_hi, :], pad], axis=0))


def _regroup_call(w_in):
    _, d, n = w_in.shape
    wt = jnp.swapaxes(w_in, 1, 2)
    return pl.pallas_call(
        _regroup_kernel,
        grid=(1,),
        in_specs=[pl.BlockSpec((1, n, d), lambda i: (0, 0, 0), pipeline_mode=pl.Buffered(1))],
        out_specs=pl.BlockSpec((d, COL_LR + LANES), lambda i: (0, 0), pipeline_mode=pl.Buffered(1)),
        out_shape=jax.ShapeDtypeStruct((d, COL_LR + LANES), BF16),
        compiler_params=pltpu.CompilerParams(dimension_semantics=("arbitrary",), vmem_limit_bytes=VMEM_LIMIT),
        name="regroup",
    )(wt)


def _conv_block(ext_ref, wdw_ref, cv_ref, base, lb):
    first = CONV_HALO - (CONV_WIDTH - 1)
    lanes = slice(lb * LANES, (lb + 1) * LANES)
    acc = jnp.zeros((CONV_ROWS, LANES), F32)
    for k in range(CONV_WIDTH):
        acc = acc + wdw_ref[k:k + 1, lanes] * ext_ref[lb, base + first + k:base + first + k + CONV_ROWS, :]
    cv_ref[base:base + CONV_ROWS, lanes] = acc


def _in_proj_kernel(x_ref, mod_ref, gmix_ref, win_ref, bglu_ref, wa2_ref, ba2_ref,
                    wdw_ref, bdw_ref, gln_ref, bln_ref, wco_ref, bco_ref,
                    q_ref, k_ref, v_ref, sg_ref, la_ref, gg_ref, yc_ref, ext_ref, cv_ref, hb_ref, gc_ref):
    tm = x_ref.shape[1]
    s = pl.program_id(1)
    n_lb = CONV_DIM // LANES

    @pl.when(s == 0)
    def _():
        ext_ref[:, 0:CONV_HALO, :] = jnp.zeros((n_lb, CONV_HALO, LANES), F32)

    @pl.when(s != 0)
    def _():
        ext_ref[:, 0:CONV_HALO, :] = ext_ref[:, tm:tm + CONV_HALO, :]

    x = x_ref[0]
    sh1 = mod_ref[0, 0:1, :]
    sc1 = mod_ref[0, 1:2, :]
    h = (_rms(x) * gmix_ref[...]) * (1.0 + sc1) + sh1
    hb_ref[...] = h.astype(BF16)

    def proj(lo, n):
        return jnp.dot(hb_ref[...], win_ref[:, lo:lo + n], preferred_element_type=F32)

    nb = 512
    for j in range(CONV_DIM // nb):
        lo = j * nb
        a = proj(COL_GLU + lo, nb) + bglu_ref[:, lo:lo + nb]
        b = proj(COL_GLU + CONV_DIM + lo, nb) + bglu_ref[:, CONV_DIM + lo:CONV_DIM + lo + nb]
        u = a * _sigmoid(b)
        for i in range(nb // LANES):
            ext_ref[lo // LANES + i, CONV_HALO:CONV_HALO + tm, :] = u[:, i * LANES:(i + 1) * LANES]

    def q_seg():
        q_ref[0] = proj(COL_Q, GLA_DK).astype(BF16)

    def k_seg():
        k_ref[0] = proj(COL_K, GLA_DK).astype(BF16)

    def v_seg(lo):
        v_ref[0, :, lo:lo + nb] = proj(COL_V + lo, nb).astype(BF16)

    def g_seg(lo):
        g = proj(COL_G + lo, nb)
        sg_ref[0, :, lo:lo + nb] = (g * _sigmoid(g)).astype(BF16)

    def la_seg():
        p_lr = proj(COL_LR, LANES).astype(BF16)
        z = jnp.dot(p_lr, wa2_ref[...], preferred_element_type=F32) + ba2_ref[...]
        la_ref[0] = (jnp.minimum(z, 0.0) - jnp.log(1.0 + jnp.exp(-jnp.abs(z)))) * (1.0 / GATE_TAU)

    def gg_seg(lo):
        gg_ref[0, :, lo:lo + nb] = _sigmoid(proj(COL_GATE + D_MODEL + lo, nb)).astype(BF16)

    def gc_seg(lo):
        gc_ref[:, lo:lo + nb] = _sigmoid(proj(COL_GATE + lo, nb)).astype(BF16)

    segments = [q_seg, k_seg, functools.partial(v_seg, 0), functools.partial(v_seg, nb),
                functools.partial(g_seg, 0), functools.partial(g_seg, nb), la_seg,
                functools.partial(gg_seg, 0), functools.partial(gg_seg, nb),
                functools.partial(gc_seg, 0), functools.partial(gc_seg, nb)]
    conv_blocks = [(r * CONV_ROWS, lb) for r in range(tm // CONV_ROWS) for lb in range(n_lb)]
    bounds = [round(i * len(conv_blocks) / len(segments)) for i in range(len(segments) + 1)]
    for i, seg in enumerate(segments):
        seg()
        for base, lb in conv_blocks[bounds[i]:bounds[i + 1]]:
            _conv_block(ext_ref, wdw_ref, cv_ref, base, lb)

    cv = cv_ref[...] + bdw_ref[...]
    mu = jnp.mean(cv, axis=-1, keepdims=True)
    cen = cv - mu
    var = jnp.mean(cen * cen, axis=-1, keepdims=True)
    yn = cen * lax.rsqrt(var + EPS) * gln_ref[...] + bln_ref[...]
    act = (yn * _sigmoid(yn)).astype(BF16)
    y = jnp.dot(act, wco_ref[...], preferred_element_type=F32) + bco_ref[...]
    yc_ref[0] = (gc_ref[...].astype(F32) * y).astype(BF16)


def _in_proj_call(x, mod3, g_mix, win, bglu, wa2, ba2, wdw, bdw, gln, bln, wco, bco, tm):
    B, S, D = x.shape
    tok = lambda n: pl.BlockSpec((1, tm, n), lambda b, s: (b, s, 0))
    out_dims = (GLA_DK, GLA_DK, GLA_DV, GLA_DV, GLA_DK, D_MODEL, D_MODEL)
    out_dtypes = (BF16, BF16, BF16, BF16, F32, BF16, BF16)
    consts = (g_mix, win, bglu, wa2, ba2, wdw, bdw, gln, bln, wco, bco)
    return pl.pallas_call(
        _in_proj_kernel,
        grid=(B, S // tm),
        in_specs=[tok(D), pl.BlockSpec((1, 6, D), lambda b, s: (b, 0, 0))] + [_const_spec(a.shape) for a in consts],
        out_specs=[tok(n) for n in out_dims],
        out_shape=[jax.ShapeDtypeStruct((B, S, n), dt) for n, dt in zip(out_dims, out_dtypes)],
        scratch_shapes=[pltpu.VMEM((CONV_DIM // LANES, tm + CONV_HALO, LANES), F32),
                        pltpu.VMEM((tm, CONV_DIM), F32),
                        pltpu.VMEM((tm, D_MODEL), BF16), pltpu.VMEM((tm, D_MODEL), BF16)],
        compiler_params=pltpu.CompilerParams(dimension_semantics=("arbitrary", "arbitrary"),
                                             vmem_limit_bytes=VMEM_LIMIT),
        name="in_proj",
    )(x, mod3, *consts)


def _split3(x):
    hi = x.astype(BF16)
    r1 = x - hi.astype(F32)
    mid = r1.astype(BF16)
    lo = (r1 - mid.astype(F32)).astype(BF16)
    return hi, mid, lo


def _gla_tile(bi, q_ref, k_ref, v_ref, la_ref, sg_ref, gg_ref, yc_ref, sel_ref, ggla_ref, wout_ref, m_ref,
              state_ref, b_ref, on_ref):
    tg = q_ref.shape[1]
    n_chunks = tg // CHUNK

    b_ref[bi] = sum(jnp.dot(sel_ref[...], p, preferred_element_type=F32) for p in _split3(la_ref[bi]))

    kdec, qb, qm, km, dec = [], [], [], [], []
    for c in range(n_chunks):
        rows = slice(c * CHUNK, (c + 1) * CHUNK)
        b = b_ref[bi, rows, :]
        b_end = b_ref[bi, (c + 1) * CHUNK - 1:(c + 1) * CHUNK, :]
        b_mid = b_ref[bi, c * CHUNK + CHUNK // 2:c * CHUNK + CHUNK // 2 + 1, :]
        q = q_ref[bi, rows, :].astype(F32) * (HEAD_K ** -0.5)
        k = k_ref[bi, rows, :].astype(F32)
        kdec.append((k * jnp.exp(b_end - b)).astype(BF16))
        qb.append((q * jnp.exp(b)).astype(BF16))
        qm.append((q * jnp.exp(b - b_mid)).astype(BF16))
        km.append((k * jnp.exp(b_mid - b)).astype(BF16))
        dec.append(jnp.exp(b_end))
    qm = jnp.concatenate(qm, axis=0)
    km = jnp.concatenate(km, axis=0)

    ri = lax.broadcasted_iota(jnp.int32, (tg, tg), 0)
    ci = lax.broadcasted_iota(jnp.int32, (tg, tg), 1)
    keep = ((ri // CHUNK) == (ci // CHUNK)) & (ci <= ri)
    nt = (((1,), (1,)), ((), ()))
    tn = (((0,), (0,)), ((), ()))

    for h in range(GLA_HEADS):
        kc = slice(h * HEAD_K, (h + 1) * HEAD_K)
        vc = slice(h * HEAD_V, (h + 1) * HEAD_V)
        vh = v_ref[bi, :, vc]
        scores = lax.dot_general(qm[:, kc], km[:, kc], nt, preferred_element_type=F32)
        o = jnp.dot(jnp.where(keep, scores, 0.0).astype(BF16), vh, preferred_element_type=F32)
        st = state_ref[bi, h]
        inter = []
        for c in range(n_chunks):
            rows = slice(c * CHUNK, (c + 1) * CHUNK)
            inter.append(lax.dot_general(qb[c][:, kc], st.astype(BF16), nt, preferred_element_type=F32))
            kv = lax.dot_general(vh[rows, :], kdec[c][:, kc], tn, preferred_element_type=F32)
            st = st * dec[c][:, kc] + kv
        state_ref[bi, h] = st
        o = o + jnp.concatenate(inter, axis=0)
        on = _rms(o) * ggla_ref[:, vc] * sg_ref[bi, :, vc].astype(F32)
        on_ref[bi, :, vc] = on.astype(BF16)

    y = jnp.dot(on_ref[bi], wout_ref[...], preferred_element_type=F32)
    m_ref[bi] = (yc_ref[bi].astype(F32) + gg_ref[bi].astype(F32) * y).astype(BF16)


def _gla_kernel(q_ref, k_ref, v_ref, la_ref, sg_ref, gg_ref, yc_ref, sel_ref, ggla_ref, wout_ref, m_ref,
                state_ref, b_ref, on_ref):
    @pl.when(pl.program_id(1) == 0)
    def _():
        state_ref[...] = jnp.zeros_like(state_ref)

    for bi in range(q_ref.shape[0]):
        _gla_tile(bi, q_ref, k_ref, v_ref, la_ref, sg_ref, gg_ref, yc_ref, sel_ref, ggla_ref, wout_ref, m_ref,
                  state_ref, b_ref, on_ref)


def _chunk_selectors(tg):
    row = np.arange(tg)[:, None]
    col = np.arange(tg)[None, :]
    return jnp.asarray(((row // CHUNK) == (col // CHUNK)) & (col <= row), dtype=BF16)


def _gla_call(q, k, v, la, sg, gg, yc, ggla, wout, tg, nbatch):
    B, S, _ = q.shape
    tok = lambda n: pl.BlockSpec((nbatch, tg, n), lambda b, s: (b, s, 0))
    consts = (_chunk_selectors(tg), ggla, wout)
    return pl.pallas_call(
        _gla_kernel,
        grid=(B // nbatch, S // tg),
        in_specs=[tok(GLA_DK), tok(GLA_DK), tok(GLA_DV), tok(GLA_DK), tok(GLA_DV), tok(D_MODEL), tok(D_MODEL)]
                 + [_const_spec(a.shape) for a in consts],
        out_specs=tok(D_MODEL),
        out_shape=jax.ShapeDtypeStruct((B, S, D_MODEL), BF16),
        scratch_shapes=[pltpu.VMEM((nbatch, GLA_HEADS, HEAD_V, HEAD_K), F32),
                        pltpu.VMEM((nbatch, tg, GLA_DK), F32), pltpu.VMEM((nbatch, tg, GLA_DV), BF16)],
        compiler_params=pltpu.CompilerParams(dimension_semantics=("arbitrary", "arbitrary"),
                                             vmem_limit_bytes=VMEM_LIMIT),
        name="gla",
    )(q, k, v, la, sg, gg, yc, *consts)


def _mlp_kernel(x_ref, m_ref, mod_ref, wo_ref, gmlp_ref, w1_ref, w2_ref, gfin_ref, o_ref):
    gt1 = mod_ref[0, 2:3, :]
    sh2 = mod_ref[0, 3:4, :]
    sc2 = mod_ref[0, 4:5, :]
    gt2 = mod_ref[0, 5:6, :]
    x1 = x_ref[0] + gt1 * jnp.dot(m_ref[0], wo_ref[...], preferred_element_type=F32)
    h2 = ((_rms(x1) * gmlp_ref[...]) * (1.0 + sc2) + sh2).astype(BF16)
    nb = 1024
    f = jnp.zeros_like(x1)
    for j in range(D_FF // nb):
        t = jnp.dot(h2, w1_ref[:, j * nb:(j + 1) * nb], preferred_element_type=F32)
        t = jnp.maximum(t, 0.0)
        f = f + jnp.dot((t * t).astype(BF16), w2_ref[j * nb:(j + 1) * nb, :], preferred_element_type=F32)
    x2 = x1 + gt2 * f
    o_ref[0] = _rms(x2) * gfin_ref[...]


def _mlp_call(x, m, mod3, wo, gmlp, w1, w2, gfin, tm):
    B, S, D = x.shape
    tok = pl.BlockSpec((1, tm, D), lambda b, s: (b, s, 0))
    consts_a = (wo, gmlp, w1, w2, gfin)
    return pl.pallas_call(
        _mlp_kernel,
        grid=(B, S // tm),
        in_specs=[tok, tok, pl.BlockSpec((1, 6, D), lambda b, s: (b, 0, 0))] + [_const_spec(a.shape) for a in consts_a],
        out_specs=tok,
        out_shape=jax.ShapeDtypeStruct((B, S, D), F32),
        compiler_params=pltpu.CompilerParams(dimension_semantics=("arbitrary", "arbitrary"),
                                             vmem_limit_bytes=VMEM_LIMIT),
        name="mlp",
    )(x, m, mod3, *consts_a)


PROJ_TOKENS = 512
GLA_TOKENS = 256
GLA_SEQS = (4, 2, 1)


def _tile(S, want):
    t = min(S, want)
    assert S % t == 0 and t % CHUNK == 0, (S, t)
    return t


def kernel(x, c, w_ada, b_ada, g_mix, w_in, b_glu, w_dw, b_dw, g_cln, b_cln, w_conv_out, b_conv_out,
           w_a2, b_a2, g_gla, w_gla_out, w_o, g_mlp, w_ff1, w_ff2, g_final):
    B, S, D = x.shape
    assert D == D_MODEL and w_ada.shape[0] == 1
    row = lambda a: a.reshape(1, -1)

    mod3 = _mod_call(c, w_ada[0], b_ada[0]).reshape(B, 6, D)

    win = _regroup_call(w_in)
    wa2 = jnp.pad(w_a2[0].astype(BF16), ((0, LANES - GATE_RANK), (0, 0)))

    q, k, v, sg, la, gg, yc = _in_proj_call(
        x, mod3, row(g_mix[0]), win, row(b_glu[0]), wa2, row(b_a2[0]),
        w_dw[0], row(b_dw[0]), row(g_cln[0]), row(b_cln[0]), w_conv_out[0].astype(BF16), row(b_conv_out[0]),
        tm=_tile(S, PROJ_TOKENS))
    m = _gla_call(q, k, v, la, sg, gg, yc, row(g_gla[0]), w_gla_out[0].astype(BF16), tg=_tile(S, GLA_TOKENS),
                  nbatch=next(n for n in GLA_SEQS if B % n == 0))
    return _mlp_call(x, m, mod3, w_o[0].astype(BF16), row(g_mlp[0]), w_ff1[0].astype(BF16),
                     w_ff2[0].astype(BF16), row(g_final), tm=_tile(S, PROJ_TOKENS))
```

```python
import functools

import jax
import jax.numpy as jnp
import numpy as np
from jax import lax
from jax.experimental import pallas as pl
from jax.experimental.pallas import tpu as pltpu

D_MODEL = 1024
CONV_DIM = 1024
CONV_WIDTH = 31
GLA_HEADS = 4
GLA_DK = 512
GLA_DV = 1024
HEAD_K = GLA_DK // GLA_HEADS
HEAD_V = GLA_DV // GLA_HEADS
GATE_RANK = 16
GATE_TAU = 16.0
CHUNK = 64
D_FF = 4 * D_MODEL
EPS = 1e-6

LANES = 128
SUBLANES = 8
CONV_HALO = 32
CONV_ROWS = 64
MOD_COLS = 1024
PROJ_COLS = 512
FF_COLS = 1024
VMEM_LIMIT = 56 * 1024 * 1024

COL_GLU = 0
COL_Q = COL_GLU + 2 * CONV_DIM
COL_K = COL_Q + GLA_DK
COL_V = COL_K + GLA_DK
COL_G = COL_V + GLA_DV
COL_GATE = COL_G + GLA_DV
COL_LR = COL_GATE + 2 * D_MODEL

F32 = jnp.float32
BF16 = jnp.bfloat16


def _const_spec(shape):
    nd = len(shape)
    return pl.BlockSpec(shape, lambda *_: (0,) * nd, pipeline_mode=pl.Buffered(1))


def _rms(x):
    return x * lax.rsqrt(jnp.mean(x * x, axis=-1, keepdims=True) + EPS)


def _sigmoid(x):
    return 1.0 / (1.0 + jnp.exp(-x))


def _mod_kernel(c_ref, w_ref, b_ref, o_ref):
    c = c_ref[...]
    s = c * _sigmoid(c)
    o_ref[...] = jnp.dot(s, w_ref[...], precision=lax.Precision.HIGHEST,
                         preferred_element_type=F32) + b_ref[...]


def _mod_call(c, w_ada, b_ada):
    B = c.shape[0]
    n = w_ada.shape[1]
    bn = MOD_COLS
    return pl.pallas_call(
        _mod_kernel,
        grid=(n // bn,),
        in_specs=[pl.BlockSpec((B, D_MODEL), lambda j: (0, 0)),
                  pl.BlockSpec((D_MODEL, bn), lambda j: (0, j)),
                  pl.BlockSpec((1, bn), lambda j: (0, j))],
        out_specs=pl.BlockSpec((B, bn), lambda j: (0, j)),
        out_shape=jax.ShapeDtypeStruct((B, n), F32),
        name="mod",
    )(c, w_ada, b_ada.reshape(1, n))


def _regroup_kernel(wt_ref, o_ref):
    lr_lo = COL_GATE
    lr_hi = lr_lo + GATE_RANK

    def put(col, blk):
        o_ref[:, col:col + LANES] = blk.T.astype(BF16)

    for col in range(0, lr_lo, LANES):
        put(col, wt_ref[0, col:col + LANES, :])
    for col in range(lr_lo, COL_LR, LANES):
        put(col, wt_ref[0, col + GATE_RANK:col + GATE_RANK + LANES, :])
    pad = jnp.zeros((LANES - GATE_RANK, wt_ref.shape[2]), F32)
    put(COL_LR, jnp.concatenate([wt_ref[0, lr_lo:lr_hi, :], pad], axis=0))


def _regroup_call(w_in):
    _, d, n = w_in.shape
    wt = jnp.swapaxes(w_in, 1, 2)
    return pl.pallas_call(
        _regroup_kernel,
        grid=(1,),
        in_specs=[pl.BlockSpec((1, n, d), lambda i: (0, 0, 0), pipeline_mode=pl.Buffered(1))],
        out_specs=pl.BlockSpec((d, COL_LR + LANES), lambda i: (0, 0), pipeline_mode=pl.Buffered(1)),
        out_shape=jax.ShapeDtypeStruct((d, COL_LR + LANES), BF16),
        compiler_params=pltpu.CompilerParams(dimension_semantics=("arbitrary",), vmem_limit_bytes=VMEM_LIMIT),
        name="regroup",
    )(wt)


def _conv_block(ext_ref, wdw_ref, cv_ref, base, lb):
    first = CONV_HALO - (CONV_WIDTH - 1)
    lanes = slice(lb * LANES, (lb + 1) * LANES)
    acc = jnp.zeros((CONV_ROWS, LANES), F32)
    for k in range(CONV_WIDTH):
        acc = acc + wdw_ref[k:k + 1, lanes] * ext_ref[lb, base + first + k:base + first + k + CONV_ROWS, :]
    cv_ref[base:base + CONV_ROWS, lanes] = acc


def _in_proj_kernel(x_ref, mod_ref, gmix_ref, win_ref, bglu_ref, wa2_ref, ba2_ref,
                    wdw_ref, bdw_ref, gln_ref, bln_ref, wco_ref, bco_ref,
                    q_ref, k_ref, v_ref, sg_ref, la_ref, gg_ref, yc_ref, ext_ref, cv_ref, hb_ref, gc_ref):
    tm = x_ref.shape[1]
    s = pl.program_id(1)
    n_lb = CONV_DIM // LANES

    @pl.when(s == 0)
    def _():
        ext_ref[:, 0:CONV_HALO, :] = jnp.zeros((n_lb, CONV_HALO, LANES), F32)

    @pl.when(s != 0)
    def _():
        ext_ref[:, 0:CONV_HALO, :] = ext_ref[:, tm:tm + CONV_HALO, :]

    x = x_ref[0]
    sh1 = mod_ref[0, 0:1, :]
    sc1 = mod_ref[0, 1:2, :]
    h = _rms(x) * (gmix_ref[...] * (1.0 + sc1)) + sh1
    hb_ref[...] = h.astype(BF16)

    def proj(lo, n):
        return jnp.dot(hb_ref[...], win_ref[:, lo:lo + n], preferred_element_type=F32)

    nb = PROJ_COLS
    for j in range(CONV_DIM // nb):
        lo = j * nb
        a = proj(COL_GLU + lo, nb) + bglu_ref[:, lo:lo + nb]
        b = proj(COL_GLU + CONV_DIM + lo, nb) + bglu_ref[:, CONV_DIM + lo:CONV_DIM + lo + nb]
        u = a * _sigmoid(b)
        for i in range(nb // LANES):
            ext_ref[lo // LANES + i, CONV_HALO:CONV_HALO + tm, :] = u[:, i * LANES:(i + 1) * LANES]

    def q_seg():
        q_ref[0] = proj(COL_Q, GLA_DK).astype(BF16)

    def k_seg():
        k_ref[0] = proj(COL_K, GLA_DK).astype(BF16)

    def v_seg(lo):
        v_ref[0, :, lo:lo + nb] = proj(COL_V + lo, nb).astype(BF16)

    def g_seg(lo):
        g = proj(COL_G + lo, nb)
        sg_ref[0, :, lo:lo + nb] = (g * _sigmoid(g)).astype(BF16)

    def la_seg():
        p_lr = proj(COL_LR, LANES).astype(BF16)
        z = jnp.dot(p_lr, wa2_ref[...], preferred_element_type=F32) + ba2_ref[...]
        la_ref[0] = (jnp.minimum(z, 0.0) - jnp.log(1.0 + jnp.exp(-jnp.abs(z)))) * (1.0 / GATE_TAU)

    def gg_seg(lo):
        gg_ref[0, :, lo:lo + nb] = _sigmoid(proj(COL_GATE + D_MODEL + lo, nb)).astype(BF16)

    def gc_seg(lo):
        gc_ref[:, lo:lo + nb] = _sigmoid(proj(COL_GATE + lo, nb)).astype(BF16)

    segments = [q_seg, k_seg, functools.partial(v_seg, 0), functools.partial(v_seg, nb),
                functools.partial(g_seg, 0), functools.partial(g_seg, nb), la_seg,
                functools.partial(gg_seg, 0), functools.partial(gg_seg, nb),
                functools.partial(gc_seg, 0), functools.partial(gc_seg, nb)]
    conv_blocks = [(r * CONV_ROWS, lb) for r in range(tm // CONV_ROWS) for lb in range(n_lb)]
    bounds = [round(i * len(conv_blocks) / len(segments)) for i in range(len(segments) + 1)]
    for i, seg in enumerate(segments):
        seg()
        for base, lb in conv_blocks[bounds[i]:bounds[i + 1]]:
            _conv_block(ext_ref, wdw_ref, cv_ref, base, lb)

    cv = cv_ref[...] + bdw_ref[...]
    mu = jnp.mean(cv, axis=-1, keepdims=True)
    cen = cv - mu
    var = jnp.mean(cen * cen, axis=-1, keepdims=True)
    yn = cen * lax.rsqrt(var + EPS) * gln_ref[...] + bln_ref[...]
    act = (yn * _sigmoid(yn)).astype(BF16)
    y = jnp.dot(act, wco_ref[...], preferred_element_type=F32) + bco_ref[...]
    yc_ref[0] = (gc_ref[...].astype(F32) * y).astype(BF16)


def _in_proj_call(x, mod3, g_mix, win, bglu, wa2, ba2, wdw, bdw, gln, bln, wco, bco, tm):
    B, S, D = x.shape
    tok = lambda n: pl.BlockSpec((1, tm, n), lambda b, s: (b, s, 0))
    out_dims = (GLA_DK, GLA_DK, GLA_DV, GLA_DV, GLA_DK, D_MODEL, D_MODEL)
    out_dtypes = (BF16, BF16, BF16, BF16, F32, BF16, BF16)
    consts = (g_mix, win, bglu, wa2, ba2, wdw, bdw, gln, bln, wco, bco)
    return pl.pallas_call(
        _in_proj_kernel,
        grid=(B, S // tm),
        in_specs=[tok(D), pl.BlockSpec((1, 6, D), lambda b, s: (b, 0, 0))] + [_const_spec(a.shape) for a in consts],
        out_specs=[tok(n) for n in out_dims],
        out_shape=[jax.ShapeDtypeStruct((B, S, n), dt) for n, dt in zip(out_dims, out_dtypes)],
        scratch_shapes=[pltpu.VMEM((CONV_DIM // LANES, tm + CONV_HALO, LANES), F32),
                        pltpu.VMEM((tm, CONV_DIM), F32),
                        pltpu.VMEM((tm, D_MODEL), BF16), pltpu.VMEM((tm, D_MODEL), BF16)],
        compiler_params=pltpu.CompilerParams(dimension_semantics=("arbitrary", "arbitrary"),
                                             vmem_limit_bytes=VMEM_LIMIT),
        name="in_proj",
    )(x, mod3, *consts)


def _split3(x):
    hi = x.astype(BF16)
    r1 = x - hi.astype(F32)
    mid = r1.astype(BF16)
    lo = (r1 - mid.astype(F32)).astype(BF16)
    return hi, mid, lo


def _gla_tile(bi, q_ref, k_ref, v_ref, la_ref, sg_ref, gg_ref, yc_ref, sel_ref, ggla_ref, wout_ref, m_ref,
              state_ref, b_ref, on_ref):
    tg = q_ref.shape[1]
    n_chunks = tg // CHUNK

    b_ref[bi] = sum(jnp.dot(sel_ref[...], p, preferred_element_type=F32) for p in _split3(la_ref[bi]))

    kdec, qb, qm, km, dec = [], [], [], [], []
    for c in range(n_chunks):
        rows = slice(c * CHUNK, (c + 1) * CHUNK)
        b = b_ref[bi, rows, :]
        b_end = b_ref[bi, (c + 1) * CHUNK - 1:(c + 1) * CHUNK, :]
        b_mid = b_ref[bi, c * CHUNK + CHUNK // 2:c * CHUNK + CHUNK // 2 + 1, :]
        q = q_ref[bi, rows, :].astype(F32) * (HEAD_K ** -0.5)
        k = k_ref[bi, rows, :].astype(F32)
        kdec.append((k * jnp.exp(b_end - b)).astype(BF16))
        qb.append((q * jnp.exp(b)).astype(BF16))
        qm.append((q * jnp.exp(b - b_mid)).astype(BF16))
        km.append((k * jnp.exp(b_mid - b)).astype(BF16))
        dec.append(jnp.exp(b_end))
    qm = jnp.concatenate(qm, axis=0)
    km = jnp.concatenate(km, axis=0)

    ri = lax.broadcasted_iota(jnp.int32, (tg, tg), 0)
    ci = lax.broadcasted_iota(jnp.int32, (tg, tg), 1)
    keep = ((ri // CHUNK) == (ci // CHUNK)) & (ci <= ri)
    nt = (((1,), (1,)), ((), ()))
    tn = (((0,), (0,)), ((), ()))

    for h in range(GLA_HEADS):
        kc = slice(h * HEAD_K, (h + 1) * HEAD_K)
        vc = slice(h * HEAD_V, (h + 1) * HEAD_V)
        vh = v_ref[bi, :, vc]
        scores = lax.dot_general(qm[:, kc], km[:, kc], nt, preferred_element_type=F32)
        o = jnp.dot(jnp.where(keep, scores, 0.0).astype(BF16), vh, preferred_element_type=F32)
        st = state_ref[bi, h]
        inter = []
        for c in range(n_chunks):
            rows = slice(c * CHUNK, (c + 1) * CHUNK)
            inter.append(lax.dot_general(qb[c][:, kc], st.astype(BF16), nt, preferred_element_type=F32))
            kv = lax.dot_general(vh[rows, :], kdec[c][:, kc], tn, preferred_element_type=F32)
            st = st * dec[c][:, kc] + kv
        state_ref[bi, h] = st
        o = o + jnp.concatenate(inter, axis=0)
        on = _rms(o) * ggla_ref[:, vc] * sg_ref[bi, :, vc].astype(F32)
        on_ref[bi, :, vc] = on.astype(BF16)

    y = jnp.dot(on_ref[bi], wout_ref[...], preferred_element_type=F32)
    m_ref[bi] = (yc_ref[bi].astype(F32) + gg_ref[bi].astype(F32) * y).astype(BF16)


def _gla_kernel(q_ref, k_ref, v_ref, la_ref, sg_ref, gg_ref, yc_ref, sel_ref, ggla_ref, wout_ref, m_ref,
                state_ref, b_ref, on_ref):
    @pl.when(pl.program_id(1) == 0)
    def _():
        state_ref[...] = jnp.zeros_like(state_ref)

    for bi in range(q_ref.shape[0]):
        _gla_tile(bi, q_ref, k_ref, v_ref, la_ref, sg_ref, gg_ref, yc_ref, sel_ref, ggla_ref, wout_ref, m_ref,
                  state_ref, b_ref, on_ref)


def _chunk_selectors(tg):
    row = np.arange(tg)[:, None]
    col = np.arange(tg)[None, :]
    return jnp.asarray(((row // CHUNK) == (col // CHUNK)) & (col <= row), dtype=BF16)


def _gla_call(q, k, v, la, sg, gg, yc, ggla, wout, tg, nbatch):
    B, S, _ = q.shape
    tok = lambda n: pl.BlockSpec((nbatch, tg, n), lambda b, s: (b, s, 0))
    consts = (_chunk_selectors(tg), ggla, wout)
    return pl.pallas_call(
        _gla_kernel,
        grid=(B // nbatch, S // tg),
        in_specs=[tok(GLA_DK), tok(GLA_DK), tok(GLA_DV), tok(GLA_DK), tok(GLA_DV), tok(D_MODEL), tok(D_MODEL)]
                 + [_const_spec(a.shape) for a in consts],
        out_specs=tok(D_MODEL),
        out_shape=jax.ShapeDtypeStruct((B, S, D_MODEL), BF16),
        scratch_shapes=[pltpu.VMEM((nbatch, GLA_HEADS, HEAD_V, HEAD_K), F32),
                        pltpu.VMEM((nbatch, tg, GLA_DK), F32), pltpu.VMEM((nbatch, tg, GLA_DV), BF16)],
        compiler_params=pltpu.CompilerParams(dimension_semantics=("arbitrary", "arbitrary"),
                                             vmem_limit_bytes=VMEM_LIMIT),
        name="gla",
    )(q, k, v, la, sg, gg, yc, *consts)


def _mlp_kernel(x_ref, m_ref, mod_ref, wo_ref, gmlp_ref, w1_ref, w2_ref, gfin_ref, o_ref):
    gt1 = mod_ref[0, 2:3, :]
    sh2 = mod_ref[0, 3:4, :]
    sc2 = mod_ref[0, 4:5, :]
    gt2 = mod_ref[0, 5:6, :]
    x1 = x_ref[0] + gt1 * jnp.dot(m_ref[0], wo_ref[...], preferred_element_type=F32)
    h2 = (_rms(x1) * (gmlp_ref[...] * (1.0 + sc2)) + sh2).astype(BF16)
    nb = FF_COLS
    f = jnp.zeros_like(x1)
    for j in range(D_FF // nb):
        t = jnp.dot(h2, w1_ref[:, j * nb:(j + 1) * nb], preferred_element_type=F32)
        t = jnp.maximum(t, 0.0)
        f = f + jnp.dot((t * t).astype(BF16), w2_ref[j * nb:(j + 1) * nb, :], preferred_element_type=F32)
    x2 = x1 + gt2 * f
    o_ref[0] = _rms(x2) * gfin_ref[...]


def _mlp_call(x, m, mod3, wo, gmlp, w1, w2, gfin, tm):
    B, S, D = x.shape
    tok = pl.BlockSpec((1, tm, D), lambda b, s: (b, s, 0))
    consts_a = (wo, gmlp, w1, w2, gfin)
    return pl.pallas_call(
        _mlp_kernel,
        grid=(B, S // tm),
        in_specs=[tok, tok, pl.BlockSpec((1, 6, D), lambda b, s: (b, 0, 0))] + [_const_spec(a.shape) for a in consts_a],
        out_specs=tok,
        out_shape=jax.ShapeDtypeStruct((B, S, D), F32),
        compiler_params=pltpu.CompilerParams(dimension_semantics=("arbitrary", "arbitrary"),
                                             vmem_limit_bytes=VMEM_LIMIT),
        name="mlp",
    )(x, m, mod3, *consts_a)


PROJ_TOKENS = 512
GLA_TOKENS = 256
GLA_SEQS = (4, 2, 1)


def _tile(S, want):
    t = min(S, want)
    assert S % t == 0 and t % CHUNK == 0, (S, t)
    return t


def kernel(x, c, w_ada, b_ada, g_mix, w_in, b_glu, w_dw, b_dw, g_cln, b_cln, w_conv_out, b_conv_out,
           w_a2, b_a2, g_gla, w_gla_out, w_o, g_mlp, w_ff1, w_ff2, g_final):
    B, S, D = x.shape
    assert D == D_MODEL and w_ada.shape[0] == 1
    row = lambda a: a.reshape(1, -1)

    mod3 = _mod_call(c, w_ada[0], b_ada[0]).reshape(B, 6, D)

    win = _regroup_call(w_in)
    wa2 = jnp.pad(w_a2[0].astype(BF16), ((0, LANES - GATE_RANK), (0, 0)))

    q, k, v, sg, la, gg, yc = _in_proj_call(
        x, mod3, row(g_mix[0]), win, row(b_glu[0]), wa2, row(b_a2[0]),
        w_dw[0], row(b_dw[0]), row(g_cln[0]), row(b_cln[0]), w_conv_out[0].astype(BF16), row(b_conv_out[0]),
        tm=_tile(S, PROJ_TOKENS))
    m = _gla_call(q, k, v, la, sg, gg, yc, row(g_gla[0]), w_gla_out[0].astype(BF16), tg=_tile(S, GLA_TOKENS),
                  nbatch=next(n for n in GLA_SEQS if B % n == 0))
    return _mlp_call(x, m, mod3, w_o[0].astype(BF16), row(g_mlp[0]), w_ff1[0].astype(BF16),
                     w_ff2[0].astype(BF16), row(g_final), tm=_tile(S, PROJ_TOKENS))
```

```python
import functools

import jax
import jax.numpy as jnp
import numpy as np
from jax import lax
from jax.experimental import pallas as pl
from jax.experimental.pallas import tpu as pltpu

D_MODEL = 1024
CONV_DIM = 1024
CONV_WIDTH = 31
GLA_HEADS = 4
GLA_DK = 512
GLA_DV = 1024
HEAD_K = GLA_DK // GLA_HEADS
HEAD_V = GLA_DV // GLA_HEADS
GATE_RANK = 16
GATE_TAU = 16.0
CHUNK = 64
D_FF = 4 * D_MODEL
EPS = 1e-6

LANES = 128
SUBLANES = 8
CONV_HALO = 32
CONV_ROWS = 64
MOD_COLS = 1024
PROJ_COLS = 512
FF_COLS = 1024
GLU_COLS = 256
VMEM_LIMIT = 56 * 1024 * 1024

COL_GLU = 0
COL_Q = COL_GLU + 2 * CONV_DIM
COL_K = COL_Q + GLA_DK
COL_V = COL_K + GLA_DK
COL_G = COL_V + GLA_DV
COL_GATE = COL_G + GLA_DV
COL_LR = COL_GATE + 2 * D_MODEL

F32 = jnp.float32
BF16 = jnp.bfloat16


def _const_spec(shape):
    nd = len(shape)
    return pl.BlockSpec(shape, lambda *_: (0,) * nd, pipeline_mode=pl.Buffered(1))


def _rms(x):
    return x * lax.rsqrt(jnp.mean(x * x, axis=-1, keepdims=True) + EPS)


def _sigmoid(x):
    return 1.0 / (1.0 + jnp.exp(-x))


def _mod_kernel(c_ref, w_ref, b_ref, o_ref):
    c = c_ref[...]
    s = c * _sigmoid(c)
    o_ref[...] = jnp.dot(s, w_ref[...], precision=lax.Precision.HIGHEST,
                         preferred_element_type=F32) + b_ref[...]


def _mod_call(c, w_ada, b_ada):
    B = c.shape[0]
    n = w_ada.shape[1]
    bn = MOD_COLS
    return pl.pallas_call(
        _mod_kernel,
        grid=(n // bn,),
        in_specs=[pl.BlockSpec((B, D_MODEL), lambda j: (0, 0)),
                  pl.BlockSpec((D_MODEL, bn), lambda j: (0, j)),
                  pl.BlockSpec((1, bn), lambda j: (0, j))],
        out_specs=pl.BlockSpec((B, bn), lambda j: (0, j)),
        out_shape=jax.ShapeDtypeStruct((B, n), F32),
        name="mod",
    )(c, w_ada, b_ada.reshape(1, n))


def _regroup_kernel(wt_ref, o_ref):
    lr_lo = COL_GATE
    lr_hi = lr_lo + GATE_RANK

    def put(col, blk):
        o_ref[:, col:col + LANES] = blk.T.astype(BF16)

    for col in range(0, lr_lo, LANES):
        put(col, wt_ref[0, col:col + LANES, :])
    for col in range(lr_lo, COL_LR, LANES):
        put(col, wt_ref[0, col + GATE_RANK:col + GATE_RANK + LANES, :])
    pad = jnp.zeros((LANES - GATE_RANK, wt_ref.shape[2]), F32)
    put(COL_LR, jnp.concatenate([wt_ref[0, lr_lo:lr_hi, :], pad], axis=0))


def _regroup_call(w_in):
    _, d, n = w_in.shape
    wt = jnp.swapaxes(w_in, 1, 2)
    return pl.pallas_call(
        _regroup_kernel,
        grid=(1,),
        in_specs=[pl.BlockSpec((1, n, d), lambda i: (0, 0, 0), pipeline_mode=pl.Buffered(1))],
        out_specs=pl.BlockSpec((d, COL_LR + LANES), lambda i: (0, 0), pipeline_mode=pl.Buffered(1)),
        out_shape=jax.ShapeDtypeStruct((d, COL_LR + LANES), BF16),
        compiler_params=pltpu.CompilerParams(dimension_semantics=("arbitrary",), vmem_limit_bytes=VMEM_LIMIT),
        name="regroup",
    )(wt)


def _conv_block(ext_ref, wdw_ref, cv_ref, base, lb):
    first = CONV_HALO - (CONV_WIDTH - 1)
    lanes = slice(lb * LANES, (lb + 1) * LANES)
    acc = jnp.zeros((CONV_ROWS, LANES), F32)
    for k in range(CONV_WIDTH):
        acc = acc + wdw_ref[k:k + 1, lanes] * ext_ref[lb, base + first + k:base + first + k + CONV_ROWS, :]
    cv_ref[base:base + CONV_ROWS, lanes] = acc


def _in_proj_kernel(x_ref, mod_ref, gmix_ref, win_ref, bglu_ref, wa2_ref, ba2_ref,
                    wdw_ref, bdw_ref, gln_ref, bln_ref, wco_ref, bco_ref,
                    q_ref, k_ref, v_ref, sg_ref, la_ref, gg_ref, yc_ref, ext_ref, cv_ref, hb_ref, gc_ref):
    tm = x_ref.shape[1]
    s = pl.program_id(1)
    n_lb = CONV_DIM // LANES

    @pl.when(s == 0)
    def _():
        ext_ref[:, 0:CONV_HALO, :] = jnp.zeros((n_lb, CONV_HALO, LANES), F32)

    @pl.when(s != 0)
    def _():
        ext_ref[:, 0:CONV_HALO, :] = ext_ref[:, tm:tm + CONV_HALO, :]

    x = x_ref[0]
    sh1 = mod_ref[0, 0:1, :]
    sc1 = mod_ref[0, 1:2, :]
    h = _rms(x) * (gmix_ref[...] * (1.0 + sc1)) + sh1
    hb_ref[...] = h.astype(BF16)

    def proj(lo, n):
        return jnp.dot(hb_ref[...], win_ref[:, lo:lo + n], preferred_element_type=F32)

    nb = PROJ_COLS
    ng = GLU_COLS
    for j in range(CONV_DIM // ng):
        lo = j * ng
        a = proj(COL_GLU + lo, ng) + bglu_ref[:, lo:lo + ng]
        b = proj(COL_GLU + CONV_DIM + lo, ng) + bglu_ref[:, CONV_DIM + lo:CONV_DIM + lo + ng]
        u = a * _sigmoid(b)
        for i in range(ng // LANES):
            ext_ref[lo // LANES + i, CONV_HALO:CONV_HALO + tm, :] = u[:, i * LANES:(i + 1) * LANES]

    def q_seg():
        q_ref[0] = proj(COL_Q, GLA_DK).astype(BF16)

    def k_seg():
        k_ref[0] = proj(COL_K, GLA_DK).astype(BF16)

    def v_seg(lo):
        v_ref[0, :, lo:lo + nb] = proj(COL_V + lo, nb).astype(BF16)

    def g_seg(lo):
        g = proj(COL_G + lo, nb)
        sg_ref[0, :, lo:lo + nb] = (g * _sigmoid(g)).astype(BF16)

    def la_seg():
        p_lr = proj(COL_LR, LANES).astype(BF16)
        z = jnp.dot(p_lr, wa2_ref[...], preferred_element_type=F32) + ba2_ref[...]
        la_ref[0] = (jnp.minimum(z, 0.0) - jnp.log(1.0 + jnp.exp(-jnp.abs(z)))) * (1.0 / GATE_TAU)

    def gg_seg(lo):
        gg_ref[0, :, lo:lo + nb] = _sigmoid(proj(COL_GATE + D_MODEL + lo, nb)).astype(BF16)

    def gc_seg(lo):
        gc_ref[:, lo:lo + nb] = _sigmoid(proj(COL_GATE + lo, nb)).astype(BF16)

    segments = [q_seg, k_seg, functools.partial(v_seg, 0), functools.partial(v_seg, nb),
                functools.partial(g_seg, 0), functools.partial(g_seg, nb), la_seg,
                functools.partial(gg_seg, 0), functools.partial(gg_seg, nb),
                functools.partial(gc_seg, 0), functools.partial(gc_seg, nb)]
    conv_blocks = [(r * CONV_ROWS, lb) for r in range(tm // CONV_ROWS) for lb in range(n_lb)]
    bounds = [round(i * len(conv_blocks) / len(segments)) for i in range(len(segments) + 1)]
    for i, seg in enumerate(segments):
        seg()
        for base, lb in conv_blocks[bounds[i]:bounds[i + 1]]:
            _conv_block(ext_ref, wdw_ref, cv_ref, base, lb)

    cv = cv_ref[...] + bdw_ref[...]
    mu = jnp.mean(cv, axis=-1, keepdims=True)
    cen = cv - mu
    var = jnp.mean(cen * cen, axis=-1, keepdims=True)
    yn = cen * lax.rsqrt(var + EPS) * gln_ref[...] + bln_ref[...]
    act = (yn * _sigmoid(yn)).astype(BF16)
    y = jnp.dot(act, wco_ref[...], preferred_element_type=F32) + bco_ref[...]
    yc_ref[0] = (gc_ref[...].astype(F32) * y).astype(BF16)


def _in_proj_call(x, mod3, g_mix, win, bglu, wa2, ba2, wdw, bdw, gln, bln, wco, bco, tm):
    B, S, D = x.shape
    tok = lambda n: pl.BlockSpec((1, tm, n), lambda b, s: (b, s, 0))
    out_dims = (GLA_DK, GLA_DK, GLA_DV, GLA_DV, GLA_DK, D_MODEL, D_MODEL)
    out_dtypes = (BF16, BF16, BF16, BF16, F32, BF16, BF16)
    consts = (g_mix, win, bglu, wa2, ba2, wdw, bdw, gln, bln, wco, bco)
    return pl.pallas_call(
        _in_proj_kernel,
        grid=(B, S // tm),
        in_specs=[tok(D), pl.BlockSpec((1, 6, D), lambda b, s: (b, 0, 0))] + [_const_spec(a.shape) for a in consts],
        out_specs=[tok(n) for n in out_dims],
        out_shape=[jax.ShapeDtypeStruct((B, S, n), dt) for n, dt in zip(out_dims, out_dtypes)],
        scratch_shapes=[pltpu.VMEM((CONV_DIM // LANES, tm + CONV_HALO, LANES), F32),
                        pltpu.VMEM((tm, CONV_DIM), F32),
                        pltpu.VMEM((tm, D_MODEL), BF16), pltpu.VMEM((tm, D_MODEL), BF16)],
        compiler_params=pltpu.CompilerParams(dimension_semantics=("arbitrary", "arbitrary"),
                                             vmem_limit_bytes=VMEM_LIMIT),
        name="in_proj",
    )(x, mod3, *consts)


def _split3(x):
    hi = x.astype(BF16)
    r1 = x - hi.astype(F32)
    mid = r1.astype(BF16)
    lo = (r1 - mid.astype(F32)).astype(BF16)
    return hi, mid, lo


def _gla_tile(bi, q_ref, k_ref, v_ref, la_ref, sg_ref, gg_ref, yc_ref, sel_ref, ggla_ref, wout_ref, m_ref,
              state_ref, b_ref, on_ref):
    tg = q_ref.shape[1]
    n_chunks = tg // CHUNK

    b_ref[bi] = sum(jnp.dot(sel_ref[...], p, preferred_element_type=F32) for p in _split3(la_ref[bi]))

    kdec, qb, qm, km, dec = [], [], [], [], []
    for c in range(n_chunks):
        rows = slice(c * CHUNK, (c + 1) * CHUNK)
        b = b_ref[bi, rows, :]
        b_end = b_ref[bi, (c + 1) * CHUNK - 1:(c + 1) * CHUNK, :]
        b_mid = b_ref[bi, c * CHUNK + CHUNK // 2:c * CHUNK + CHUNK // 2 + 1, :]
        q = q_ref[bi, rows, :].astype(F32) * (HEAD_K ** -0.5)
        k = k_ref[bi, rows, :].astype(F32)
        kdec.append((k * jnp.exp(b_end - b)).astype(BF16))
        qb.append((q * jnp.exp(b)).astype(BF16))
        qm.append((q * jnp.exp(b - b_mid)).astype(BF16))
        km.append((k * jnp.exp(b_mid - b)).astype(BF16))
        dec.append(jnp.exp(b_end))
    qm = jnp.concatenate(qm, axis=0)
    km = jnp.concatenate(km, axis=0)

    ri = lax.broadcasted_iota(jnp.int32, (tg, tg), 0)
    ci = lax.broadcasted_iota(jnp.int32, (tg, tg), 1)
    keep = ((ri // CHUNK) == (ci // CHUNK)) & (ci <= ri)
    nt = (((1,), (1,)), ((), ()))
    tn = (((0,), (0,)), ((), ()))

    for h in range(GLA_HEADS):
        kc = slice(h * HEAD_K, (h + 1) * HEAD_K)
        vc = slice(h * HEAD_V, (h + 1) * HEAD_V)
        vh = v_ref[bi, :, vc]
        scores = lax.dot_general(qm[:, kc], km[:, kc], nt, preferred_element_type=F32)
        o = jnp.dot(jnp.where(keep, scores, 0.0).astype(BF16), vh, preferred_element_type=F32)
        st = state_ref[bi, h]
        inter = []
        for c in range(n_chunks):
            rows = slice(c * CHUNK, (c + 1) * CHUNK)
            inter.append(lax.dot_general(qb[c][:, kc], st.astype(BF16), nt, preferred_element_type=F32))
            kv = lax.dot_general(vh[rows, :], kdec[c][:, kc], tn, preferred_element_type=F32)
            st = st * dec[c][:, kc] + kv
        state_ref[bi, h] = st
        o = o + jnp.concatenate(inter, axis=0)
        on = _rms(o) * ggla_ref[:, vc] * sg_ref[bi, :, vc].astype(F32)
        on_ref[bi, :, vc] = on.astype(BF16)

    y = jnp.dot(on_ref[bi], wout_ref[...], preferred_element_type=F32)
    m_ref[bi] = (yc_ref[bi].astype(F32) + gg_ref[bi].astype(F32) * y).astype(BF16)


def _gla_kernel(q_ref, k_ref, v_ref, la_ref, sg_ref, gg_ref, yc_ref, sel_ref, ggla_ref, wout_ref, m_ref,
                state_ref, b_ref, on_ref):
    @pl.when(pl.program_id(1) == 0)
    def _():
        state_ref[...] = jnp.zeros_like(state_ref)

    for bi in range(q_ref.shape[0]):
        _gla_tile(bi, q_ref, k_ref, v_ref, la_ref, sg_ref, gg_ref, yc_ref, sel_ref, ggla_ref, wout_ref, m_ref,
                  state_ref, b_ref, on_ref)


def _chunk_selectors(tg):
    row = np.arange(tg)[:, None]
    col = np.arange(tg)[None, :]
    return jnp.asarray(((row // CHUNK) == (col // CHUNK)) & (col <= row), dtype=BF16)


def _gla_call(q, k, v, la, sg, gg, yc, ggla, wout, tg, nbatch):
    B, S, _ = q.shape
    tok = lambda n: pl.BlockSpec((nbatch, tg, n), lambda b, s: (b, s, 0))
    consts = (_chunk_selectors(tg), ggla, wout)
    return pl.pallas_call(
        _gla_kernel,
        grid=(B // nbatch, S // tg),
        in_specs=[tok(GLA_DK), tok(GLA_DK), tok(GLA_DV), tok(GLA_DK), tok(GLA_DV), tok(D_MODEL), tok(D_MODEL)]
                 + [_const_spec(a.shape) for a in consts],
        out_specs=tok(D_MODEL),
        out_shape=jax.ShapeDtypeStruct((B, S, D_MODEL), BF16),
        scratch_shapes=[pltpu.VMEM((nbatch, GLA_HEADS, HEAD_V, HEAD_K), F32),
                        pltpu.VMEM((nbatch, tg, GLA_DK), F32), pltpu.VMEM((nbatch, tg, GLA_DV), BF16)],
        compiler_params=pltpu.CompilerParams(dimension_semantics=("arbitrary", "arbitrary"),
                                             vmem_limit_bytes=VMEM_LIMIT),
        name="gla",
    )(q, k, v, la, sg, gg, yc, *consts)


def _mlp_kernel(x_ref, m_ref, mod_ref, wo_ref, gmlp_ref, w1_ref, w2_ref, gfin_ref, o_ref):
    gt1 = mod_ref[0, 2:3, :]
    sh2 = mod_ref[0, 3:4, :]
    sc2 = mod_ref[0, 4:5, :]
    gt2 = mod_ref[0, 5:6, :]
    x1 = x_ref[0] + gt1 * jnp.dot(m_ref[0], wo_ref[...], preferred_element_type=F32)
    h2 = (_rms(x1) * (gmlp_ref[...] * (1.0 + sc2)) + sh2).astype(BF16)
    nb = FF_COLS
    f = jnp.zeros_like(x1)
    for j in range(D_FF // nb):
        t = jnp.dot(h2, w1_ref[:, j * nb:(j + 1) * nb], preferred_element_type=F32)
        t = jnp.maximum(t, 0.0)
        f = f + jnp.dot((t * t).astype(BF16), w2_ref[j * nb:(j + 1) * nb, :], preferred_element_type=F32)
    x2 = x1 + gt2 * f
    o_ref[0] = _rms(x2) * gfin_ref[...]


def _mlp_call(x, m, mod3, wo, gmlp, w1, w2, gfin, tm):
    B, S, D = x.shape
    tok = pl.BlockSpec((1, tm, D), lambda b, s: (b, s, 0))
    consts_a = (wo, gmlp, w1, w2, gfin)
    return pl.pallas_call(
        _mlp_kernel,
        grid=(B, S // tm),
        in_specs=[tok, tok, pl.BlockSpec((1, 6, D), lambda b, s: (b, 0, 0))] + [_const_spec(a.shape) for a in consts_a],
        out_specs=tok,
        out_shape=jax.ShapeDtypeStruct((B, S, D), F32),
        compiler_params=pltpu.CompilerParams(dimension_semantics=("arbitrary", "arbitrary"),
                                             vmem_limit_bytes=VMEM_LIMIT),
        name="mlp",
    )(x, m, mod3, *consts_a)


PROJ_TOKENS = 512
GLA_TOKENS = 256
GLA_SEQS = (4, 2, 1)


def _tile(S, want):
    t = min(S, want)
    assert S % t == 0 and t % CHUNK == 0, (S, t)
    return t


def kernel(x, c, w_ada, b_ada, g_mix, w_in, b_glu, w_dw, b_dw, g_cln, b_cln, w_conv_out, b_conv_out,
           w_a2, b_a2, g_gla, w_gla_out, w_o, g_mlp, w_ff1, w_ff2, g_final):
    B, S, D = x.shape
    assert D == D_MODEL and w_ada.shape[0] == 1
    row = lambda a: a.reshape(1, -1)

    mod3 = _mod_call(c, w_ada[0], b_ada[0]).reshape(B, 6, D)

    win = _regroup_call(w_in)
    wa2 = jnp.pad(w_a2[0].astype(BF16), ((0, LANES - GATE_RANK), (0, 0)))

    q, k, v, sg, la, gg, yc = _in_proj_call(
        x, mod3, row(g_mix[0]), win, row(b_glu[0]), wa2, row(b_a2[0]),
        w_dw[0], row(b_dw[0]), row(g_cln[0]), row(b_cln[0]), w_conv_out[0].astype(BF16), row(b_conv_out[0]),
        tm=_tile(S, PROJ_TOKENS))
    m = _gla_call(q, k, v, la, sg, gg, yc, row(g_gla[0]), w_gla_out[0].astype(BF16), tg=_tile(S, GLA_TOKENS),
                  nbatch=next(n for n in GLA_SEQS if B % n == 0))
    return _mlp_call(x, m, mod3, w_o[0].astype(BF16), row(g_mlp[0]), w_ff1[0].astype(BF16),
                     w_ff2[0].astype(BF16), row(g_final), tm=_tile(S, PROJ_TOKENS))
```

```python
import functools

import jax
import jax.numpy as jnp
import numpy as np
from jax import lax
from jax.experimental import pallas as pl
from jax.experimental.pallas import tpu as pltpu

D_MODEL = 1024
CONV_DIM = 1024
CONV_WIDTH = 31
GLA_HEADS = 4
GLA_DK = 512
GLA_DV = 1024
HEAD_K = GLA_DK // GLA_HEADS
HEAD_V = GLA_DV // GLA_HEADS
GATE_RANK = 16
GATE_TAU = 16.0
CHUNK = 64
D_FF = 4 * D_MODEL
EPS = 1e-6

LANES = 128
SUBLANES = 8
CONV_HALO = 32
CONV_ROWS = 64
MOD_COLS = 1024
PROJ_COLS = 512
FF_COLS = 1024
GLU_COLS = 256
VMEM_LIMIT = 56 * 1024 * 1024

COL_GLU = 0
COL_Q = COL_GLU + 2 * CONV_DIM
COL_K = COL_Q + GLA_DK
COL_V = COL_K + GLA_DK
COL_G = COL_V + GLA_DV
COL_GATE = COL_G + GLA_DV
COL_LR = COL_GATE + 2 * D_MODEL

F32 = jnp.float32
BF16 = jnp.bfloat16


def _const_spec(shape):
    nd = len(shape)
    return pl.BlockSpec(shape, lambda *_: (0,) * nd, pipeline_mode=pl.Buffered(1))


def _rms(x):
    return x * lax.rsqrt(jnp.mean(x * x, axis=-1, keepdims=True) + EPS)


def _sigmoid(x):
    return 1.0 / (1.0 + jnp.exp(-x))


def _mod_kernel(c_ref, w_ref, b_ref, o_ref):
    c = c_ref[...]
    s = c * _sigmoid(c)
    o_ref[...] = jnp.dot(s, w_ref[...], precision=lax.Precision.HIGHEST,
                         preferred_element_type=F32) + b_ref[...]


def _mod_call(c, w_ada, b_ada):
    B = c.shape[0]
    n = w_ada.shape[1]
    bn = MOD_COLS
    return pl.pallas_call(
        _mod_kernel,
        grid=(n // bn,),
        in_specs=[pl.BlockSpec((B, D_MODEL), lambda j: (0, 0)),
                  pl.BlockSpec((D_MODEL, bn), lambda j: (0, j)),
                  pl.BlockSpec((1, bn), lambda j: (0, j))],
        out_specs=pl.BlockSpec((B, bn), lambda j: (0, j)),
        out_shape=jax.ShapeDtypeStruct((B, n), F32),
        name="mod",
    )(c, w_ada, b_ada.reshape(1, n))


def _regroup_kernel(wt_ref, o_ref):
    lr_lo = COL_GATE
    lr_hi = lr_lo + GATE_RANK

    def put(col, blk):
        o_ref[:, col:col + LANES] = blk.T.astype(BF16)

    for col in range(0, lr_lo, LANES):
        put(col, wt_ref[0, col:col + LANES, :])
    for col in range(lr_lo, COL_LR, LANES):
        put(col, wt_ref[0, col + GATE_RANK:col + GATE_RANK + LANES, :])
    pad = jnp.zeros((LANES - GATE_RANK, wt_ref.shape[2]), F32)
    put(COL_LR, jnp.concatenate([wt_ref[0, lr_lo:lr_hi, :], pad], axis=0))


def _regroup_call(w_in):
    _, d, n = w_in.shape
    wt = jnp.swapaxes(w_in, 1, 2)
    return pl.pallas_call(
        _regroup_kernel,
        grid=(1,),
        in_specs=[pl.BlockSpec((1, n, d), lambda i: (0, 0, 0), pipeline_mode=pl.Buffered(1))],
        out_specs=pl.BlockSpec((d, COL_LR + LANES), lambda i: (0, 0), pipeline_mode=pl.Buffered(1)),
        out_shape=jax.ShapeDtypeStruct((d, COL_LR + LANES), BF16),
        compiler_params=pltpu.CompilerParams(dimension_semantics=("arbitrary",), vmem_limit_bytes=VMEM_LIMIT),
        name="regroup",
    )(wt)


def _conv_block(ext_ref, wdw_ref, cv_ref, base, lb):
    first = CONV_HALO - (CONV_WIDTH - 1)
    lanes = slice(lb * LANES, (lb + 1) * LANES)
    acc = jnp.zeros((CONV_ROWS, LANES), F32)
    for k in range(CONV_WIDTH):
        acc = acc + wdw_ref[k:k + 1, lanes] * ext_ref[lb, base + first + k:base + first + k + CONV_ROWS, :]
    cv_ref[base:base + CONV_ROWS, lanes] = acc


def _in_proj_kernel(x_ref, mod_ref, gmix_ref, win_ref, bglu_ref, wa2_ref, ba2_ref,
                    wdw_ref, bdw_ref, gln_ref, bln_ref, wco_ref, bco_ref,
                    q_ref, k_ref, v_ref, sg_ref, la_ref, gg_ref, yc_ref, ext_ref, cv_ref, hb_ref, gc_ref):
    tm = x_ref.shape[1]
    s = pl.program_id(1)
    n_lb = CONV_DIM // LANES

    @pl.when(s == 0)
    def _():
        ext_ref[:, 0:CONV_HALO, :] = jnp.zeros((n_lb, CONV_HALO, LANES), F32)

    @pl.when(s != 0)
    def _():
        ext_ref[:, 0:CONV_HALO, :] = ext_ref[:, tm:tm + CONV_HALO, :]

    x = x_ref[0]
    sh1 = mod_ref[0, 0:1, :]
    sc1 = mod_ref[0, 1:2, :]
    h = _rms(x) * (gmix_ref[...] * (1.0 + sc1)) + sh1
    hb_ref[...] = h.astype(BF16)

    def proj(lo, n):
        return jnp.dot(hb_ref[...], win_ref[:, lo:lo + n], preferred_element_type=F32)

    nb = PROJ_COLS
    ng = GLU_COLS
    for j in range(CONV_DIM // ng):
        lo = j * ng
        a = proj(COL_GLU + lo, ng) + bglu_ref[:, lo:lo + ng]
        b = proj(COL_GLU + CONV_DIM + lo, ng) + bglu_ref[:, CONV_DIM + lo:CONV_DIM + lo + ng]
        u = a * _sigmoid(b)
        for i in range(ng // LANES):
            ext_ref[lo // LANES + i, CONV_HALO:CONV_HALO + tm, :] = u[:, i * LANES:(i + 1) * LANES]

    def q_seg():
        q_ref[0] = proj(COL_Q, GLA_DK).astype(BF16)

    def k_seg():
        k_ref[0] = proj(COL_K, GLA_DK).astype(BF16)

    def v_seg(lo):
        v_ref[0, :, lo:lo + nb] = proj(COL_V + lo, nb).astype(BF16)

    def g_seg(lo):
        g = proj(COL_G + lo, nb)
        sg_ref[0, :, lo:lo + nb] = (g * _sigmoid(g)).astype(BF16)

    def la_seg():
        p_lr = proj(COL_LR, LANES).astype(BF16)
        z = jnp.dot(p_lr, wa2_ref[...], preferred_element_type=F32) + ba2_ref[...]
        la_ref[0] = (jnp.minimum(z, 0.0) - jnp.log(1.0 + jnp.exp(-jnp.abs(z)))) * (1.0 / GATE_TAU)

    def gg_seg(lo):
        gg_ref[0, :, lo:lo + nb] = _sigmoid(proj(COL_GATE + D_MODEL + lo, nb)).astype(BF16)

    def gc_seg(lo):
        gc_ref[:, lo:lo + nb] = _sigmoid(proj(COL_GATE + lo, nb)).astype(BF16)

    segments = [q_seg, k_seg, functools.partial(v_seg, 0), functools.partial(v_seg, nb),
                functools.partial(g_seg, 0), functools.partial(g_seg, nb), la_seg,
                functools.partial(gg_seg, 0), functools.partial(gg_seg, nb),
                functools.partial(gc_seg, 0), functools.partial(gc_seg, nb)]
    conv_blocks = [(r * CONV_ROWS, lb) for r in range(tm // CONV_ROWS) for lb in range(n_lb)]
    bounds = [round(i * len(conv_blocks) / len(segments)) for i in range(len(segments) + 1)]
    for i, seg in enumerate(segments):
        seg()
        for base, lb in conv_blocks[bounds[i]:bounds[i + 1]]:
            _conv_block(ext_ref, wdw_ref, cv_ref, base, lb)

    cv = cv_ref[...] + bdw_ref[...]
    mu = jnp.mean(cv, axis=-1, keepdims=True)
    cen = cv - mu
    var = jnp.mean(cen * cen, axis=-1, keepdims=True)
    yn = cen * lax.rsqrt(var + EPS) * gln_ref[...] + bln_ref[...]
    act = (yn * _sigmoid(yn)).astype(BF16)
    y = jnp.dot(act, wco_ref[...], preferred_element_type=F32) + bco_ref[...]
    yc_ref[0] = (gc_ref[...].astype(F32) * y).astype(BF16)


def _in_proj_call(x, mod3, g_mix, win, bglu, wa2, ba2, wdw, bdw, gln, bln, wco, bco, tm):
    B, S, D = x.shape
    tok = lambda n: pl.BlockSpec((1, tm, n), lambda b, s: (b, s, 0))
    out_dims = (GLA_DK, GLA_DK, GLA_DV, GLA_DV, GLA_DK, D_MODEL, D_MODEL)
    out_dtypes = (BF16, BF16, BF16, BF16, F32, BF16, BF16)
    consts = (g_mix, win, bglu, wa2, ba2, wdw, bdw, gln, bln, wco, bco)
    return pl.pallas_call(
        _in_proj_kernel,
        grid=(B, S // tm),
        in_specs=[tok(D), pl.BlockSpec((1, 6, D), lambda b, s: (b, 0, 0))] + [_const_spec(a.shape) for a in consts],
        out_specs=[tok(n) for n in out_dims],
        out_shape=[jax.ShapeDtypeStruct((B, S, n), dt) for n, dt in zip(out_dims, out_dtypes)],
        scratch_shapes=[pltpu.VMEM((CONV_DIM // LANES, tm + CONV_HALO, LANES), F32),
                        pltpu.VMEM((tm, CONV_DIM), F32),
                        pltpu.VMEM((tm, D_MODEL), BF16), pltpu.VMEM((tm, D_MODEL), BF16)],
        compiler_params=pltpu.CompilerParams(dimension_semantics=("arbitrary", "arbitrary"),
                                             vmem_limit_bytes=VMEM_LIMIT),
        name="in_proj",
    )(x, mod3, *consts)


def _split3(x):
    hi = x.astype(BF16)
    r1 = x - hi.astype(F32)
    mid = r1.astype(BF16)
    lo = (r1 - mid.astype(F32)).astype(BF16)
    return hi, mid, lo


def _gla_tile(bi, q_ref, k_ref, v_ref, la_ref, sg_ref, sel_ref, ggla_ref, m_ref, state_ref, b_ref):
    tg = q_ref.shape[1]
    n_chunks = tg // CHUNK

    b_ref[bi] = sum(jnp.dot(sel_ref[...], p, preferred_element_type=F32) for p in _split3(la_ref[bi]))

    kdec, qb, qm, km, dec = [], [], [], [], []
    for c in range(n_chunks):
        rows = slice(c * CHUNK, (c + 1) * CHUNK)
        b = b_ref[bi, rows, :]
        b_end = b_ref[bi, (c + 1) * CHUNK - 1:(c + 1) * CHUNK, :]
        b_mid = b_ref[bi, c * CHUNK + CHUNK // 2:c * CHUNK + CHUNK // 2 + 1, :]
        q = q_ref[bi, rows, :].astype(F32) * (HEAD_K ** -0.5)
        k = k_ref[bi, rows, :].astype(F32)
        kdec.append((k * jnp.exp(b_end - b)).astype(BF16))
        qb.append((q * jnp.exp(b)).astype(BF16))
        qm.append((q * jnp.exp(b - b_mid)).astype(BF16))
        km.append((k * jnp.exp(b_mid - b)).astype(BF16))
        dec.append(jnp.exp(b_end))
    qm = jnp.concatenate(qm, axis=0)
    km = jnp.concatenate(km, axis=0)

    ri = lax.broadcasted_iota(jnp.int32, (tg, tg), 0)
    ci = lax.broadcasted_iota(jnp.int32, (tg, tg), 1)
    keep = ((ri // CHUNK) == (ci // CHUNK)) & (ci <= ri)
    nt = (((1,), (1,)), ((), ()))
    tn = (((0,), (0,)), ((), ()))

    for h in range(GLA_HEADS):
        kc = slice(h * HEAD_K, (h + 1) * HEAD_K)
        vc = slice(h * HEAD_V, (h + 1) * HEAD_V)
        vh = v_ref[bi, :, vc]
        scores = lax.dot_general(qm[:, kc], km[:, kc], nt, preferred_element_type=F32)
        o = jnp.dot(jnp.where(keep, scores, 0.0).astype(BF16), vh, preferred_element_type=F32)
        st = state_ref[bi, h]
        inter = []
        for c in range(n_chunks):
            rows = slice(c * CHUNK, (c + 1) * CHUNK)
            inter.append(lax.dot_general(qb[c][:, kc], st.astype(BF16), nt, preferred_element_type=F32))
            kv = lax.dot_general(vh[rows, :], kdec[c][:, kc], tn, preferred_element_type=F32)
            st = st * dec[c][:, kc] + kv
        state_ref[bi, h] = st
        o = o + jnp.concatenate(inter, axis=0)
        on = _rms(o) * ggla_ref[:, vc] * sg_ref[bi, :, vc].astype(F32)
        m_ref[bi, :, vc] = on.astype(BF16)


def _gla_kernel(q_ref, k_ref, v_ref, la_ref, sg_ref, sel_ref, ggla_ref, m_ref, state_ref, b_ref):
    @pl.when(pl.program_id(1) == 0)
    def _():
        state_ref[...] = jnp.zeros_like(state_ref)

    for bi in range(q_ref.shape[0]):
        _gla_tile(bi, q_ref, k_ref, v_ref, la_ref, sg_ref, sel_ref, ggla_ref, m_ref, state_ref, b_ref)


def _chunk_selectors(tg):
    row = np.arange(tg)[:, None]
    col = np.arange(tg)[None, :]
    return jnp.asarray(((row // CHUNK) == (col // CHUNK)) & (col <= row), dtype=BF16)


def _gla_call(q, k, v, la, sg, ggla, tg, nbatch):
    B, S, _ = q.shape
    tok = lambda n: pl.BlockSpec((nbatch, tg, n), lambda b, s: (b, s, 0))
    consts = (_chunk_selectors(tg), ggla)
    return pl.pallas_call(
        _gla_kernel,
        grid=(B // nbatch, S // tg),
        in_specs=[tok(GLA_DK), tok(GLA_DK), tok(GLA_DV), tok(GLA_DK), tok(GLA_DV)]
                 + [_const_spec(a.shape) for a in consts],
        out_specs=tok(GLA_DV),
        out_shape=jax.ShapeDtypeStruct((B, S, GLA_DV), BF16),
        scratch_shapes=[pltpu.VMEM((nbatch, GLA_HEADS, HEAD_V, HEAD_K), F32),
                        pltpu.VMEM((nbatch, tg, GLA_DK), F32)],
        compiler_params=pltpu.CompilerParams(dimension_semantics=("arbitrary", "arbitrary"),
                                             vmem_limit_bytes=VMEM_LIMIT),
        name="gla",
    )(q, k, v, la, sg, *consts)


def _mlp_kernel(x_ref, on_ref, yc_ref, gg_ref, mod_ref, wgo_ref, wo_ref, gmlp_ref, w1_ref, w2_ref, gfin_ref, o_ref):
    gt1 = mod_ref[0, 2:3, :]
    sh2 = mod_ref[0, 3:4, :]
    sc2 = mod_ref[0, 4:5, :]
    gt2 = mod_ref[0, 5:6, :]
    y_gla = jnp.dot(on_ref[0], wgo_ref[...], preferred_element_type=F32)
    merged = (yc_ref[0].astype(F32) + gg_ref[0].astype(F32) * y_gla).astype(BF16)
    x1 = x_ref[0] + gt1 * jnp.dot(merged, wo_ref[...], preferred_element_type=F32)
    h2 = (_rms(x1) * (gmlp_ref[...] * (1.0 + sc2)) + sh2).astype(BF16)
    nb = FF_COLS
    f = jnp.zeros_like(x1)
    for j in range(D_FF // nb):
        t = jnp.dot(h2, w1_ref[:, j * nb:(j + 1) * nb], preferred_element_type=F32)
        t = jnp.maximum(t, 0.0)
        f = f + jnp.dot((t * t).astype(BF16), w2_ref[j * nb:(j + 1) * nb, :], preferred_element_type=F32)
    x2 = x1 + gt2 * f
    o_ref[0] = _rms(x2) * gfin_ref[...]


def _mlp_call(x, on, yc, gg, mod3, wgo, wo, gmlp, w1, w2, gfin, tm):
    B, S, D = x.shape
    tok = pl.BlockSpec((1, tm, D), lambda b, s: (b, s, 0))
    consts_a = (wgo, wo, gmlp, w1, w2, gfin)
    return pl.pallas_call(
        _mlp_kernel,
        grid=(B, S // tm),
        in_specs=[tok, tok, tok, tok, pl.BlockSpec((1, 6, D), lambda b, s: (b, 0, 0))]
                 + [_const_spec(a.shape) for a in consts_a],
        out_specs=tok,
        out_shape=jax.ShapeDtypeStruct((B, S, D), F32),
        compiler_params=pltpu.CompilerParams(dimension_semantics=("arbitrary", "arbitrary"),
                                             vmem_limit_bytes=VMEM_LIMIT),
        name="mlp",
    )(x, on, yc, gg, mod3, *consts_a)


PROJ_TOKENS = 512
GLA_TOKENS = 256
GLA_SEQS = (4, 2, 1)


def _tile(S, want):
    t = min(S, want)
    assert S % t == 0 and t % CHUNK == 0, (S, t)
    return t


def kernel(x, c, w_ada, b_ada, g_mix, w_in, b_glu, w_dw, b_dw, g_cln, b_cln, w_conv_out, b_conv_out,
           w_a2, b_a2, g_gla, w_gla_out, w_o, g_mlp, w_ff1, w_ff2, g_final):
    B, S, D = x.shape
    assert D == D_MODEL and w_ada.shape[0] == 1
    row = lambda a: a.reshape(1, -1)

    mod3 = _mod_call(c, w_ada[0], b_ada[0]).reshape(B, 6, D)

    win = _regroup_call(w_in)
    wa2 = jnp.pad(w_a2[0].astype(BF16), ((0, LANES - GATE_RANK), (0, 0)))

    q, k, v, sg, la, gg, yc = _in_proj_call(
        x, mod3, row(g_mix[0]), win, row(b_glu[0]), wa2, row(b_a2[0]),
        w_dw[0], row(b_dw[0]), row(g_cln[0]), row(b_cln[0]), w_conv_out[0].astype(BF16), row(b_conv_out[0]),
        tm=_tile(S, PROJ_TOKENS))
    on = _gla_call(q, k, v, la, sg, row(g_gla[0]), tg=_tile(S, GLA_TOKENS),
                   nbatch=next(n for n in GLA_SEQS if B % n == 0))
    return _mlp_call(x, on, yc, gg, mod3, w_gla_out[0].astype(BF16), w_o[0].astype(BF16), row(g_mlp[0]), w_ff1[0].astype(BF16),
                     w_ff2[0].astype(BF16), row(g_final), tm=_tile(S, PROJ_TOKENS))
```
